```python
import math
import jax
import jax.numpy as jnp
from jax import lax
import numpy as np


D_MODEL = 2048
BATCH = 16
SEQ = 2048
DEPTH = 4

GRID_W = 64
CTX_LEN = 256

MLA_HEADS = 6
MLA_Q_LORA = 768
MLA_KV_LORA = 512
MLA_NOPE_DIM = 128
MLA_ROPE_DIM = 64
MLA_V_DIM = 128
MLA_SCALE = (MLA_NOPE_DIM + MLA_ROPE_DIM) ** -0.5
GDN_HEADS = 6
GDN_K_DIM = 128
GDN_V_DIM = 128
GDN_CONV = 5
GDN_CHUNK = 64
DIFF_HEADS = 4
DIFF_QK_DIM = 64
DIFF_V_DIM = 128
DIFF_SCALE = DIFF_QK_DIM ** -0.5

D_MIX = MLA_HEADS * MLA_V_DIM + GDN_HEADS * GDN_V_DIM + DIFF_HEADS * DIFF_V_DIM
D_FF = 4 * D_MODEL
ROPE_BASE = 10000.0
Q_BLOCK = 128
NORM_EPS = 1e-6

IN_SPLITS = (
    MLA_Q_LORA, MLA_KV_LORA, MLA_ROPE_DIM,
    GDN_HEADS * GDN_K_DIM, GDN_HEADS * GDN_K_DIM,
    GDN_HEADS * GDN_V_DIM, GDN_HEADS * GDN_V_DIM,
    4 * GDN_HEADS,
    DIFF_HEADS * 2 * DIFF_QK_DIM, DIFF_HEADS * 2 * DIFF_QK_DIM,
    DIFF_HEADS * DIFF_V_DIM,
)
D_IN = sum(IN_SPLITS)

kernel_name = 'hybrid_mla_gdn_diff_dit_trunk'


def rms_norm(x, gain):
    xf = x.astype(jnp.float32)
    y = xf * lax.rsqrt(jnp.mean(xf * xf, axis=-1, keepdims=True) + NORM_EPS)
    return (y * gain.astype(jnp.float32)).astype(x.dtype)


def modulate(h, shift, scale):
    return h * (1.0 + scale) + shift


def split_in(proj):
    points = tuple(int(s) for s in np.cumsum(IN_SPLITS)[:-1])
    return jnp.split(proj, points, axis=-1)


def axial_rope_tables(row_pos, col_pos, dim):
    quarter = dim // 4
    inv_freq = ROPE_BASE ** (-jnp.arange(quarter, dtype=jnp.float32) / quarter)
    ang_r = row_pos[:, None] * inv_freq[None, :]
    ang_c = col_pos[:, None] * inv_freq[None, :]
    ang = jnp.concatenate([ang_r, ang_r, ang_c, ang_c], axis=-1)
    return jnp.cos(ang), jnp.sin(ang)


def apply_axial_rope(x, cos, sin):
    shape = (x.shape[1],) + (1,) * (x.ndim - 3) + (x.shape[-1],)
    cos = cos.reshape(shape)
    sin = sin.reshape(shape)
    a, b, cc, d = jnp.split(x, 4, axis=-1)
    rot = jnp.concatenate([-b, a, -d, cc], axis=-1)
    return (x * cos + rot * sin).astype(x.dtype)


def block_attention(q, k, v, map_weights, scale):
    B, T, H, G, d = q.shape
    dv = v.shape[-1]
    nb = T // Q_BLOCK
    qb = jnp.moveaxis(q.reshape(B, nb, Q_BLOCK, H, G, d), 1, 0)
    w = map_weights.astype(jnp.float32)

    def one_block(q_blk):
        s = jnp.einsum('bqhgd,bshgd->bhgqs', q_blk, k, preferred_element_type=jnp.float32) * scale
        p = jax.nn.softmax(s, axis=-1)
        p = jnp.einsum('bhgqs,g->bhqs', p, w)
        return jnp.einsum('bhqs,bshe->bqhe', p.astype(v.dtype), v)

    out = lax.map(one_block, qb)
    return jnp.moveaxis(out, 0, 1).reshape(B, T, H, dv)


def mla_qkv(cq, ckv, k_rope, p, rope):
    B, T, _ = cq.shape
    q = (rms_norm(cq, p['mla_q_norm_g']) @ p['mla_w_uq']).reshape(B, T, MLA_HEADS, MLA_NOPE_DIM + MLA_ROPE_DIM)
    kv = (rms_norm(ckv, p['mla_kv_norm_g']) @ p['mla_w_ukv']).reshape(B, T, MLA_HEADS, MLA_NOPE_DIM + MLA_V_DIM)
    q_nope, q_rope = q[..., :MLA_NOPE_DIM], q[..., MLA_NOPE_DIM:]
    k_nope, v = kv[..., :MLA_NOPE_DIM], kv[..., MLA_NOPE_DIM:]
    k_rope = k_rope[:, :, None, :]
    if rope is not None:
        q_rope = apply_axial_rope(q_rope, rope[0], rope[1])
        k_rope = apply_axial_rope(k_rope, rope[0], rope[1])
    q = jnp.concatenate([q_nope, q_rope], axis=-1)
    k = jnp.concatenate([k_nope, jnp.broadcast_to(k_rope, q_rope.shape)], axis=-1)
    return q[:, :, :, None, :], k[:, :, :, None, :], v


def diff_qkv(qp, kp, vp, rope):
    B, T, _ = qp.shape
    q = qp.reshape(B, T, DIFF_HEADS, 2, DIFF_QK_DIM)
    k = kp.reshape(B, T, DIFF_HEADS, 2, DIFF_QK_DIM)
    v = vp.reshape(B, T, DIFF_HEADS, DIFF_V_DIM)
    if rope is not None:
        q = apply_axial_rope(q, rope[0], rope[1])
        k = apply_axial_rope(k, rope[0], rope[1])
    return q, k, v


def centred_depthwise_conv(x, w):
    K, C = w.shape
    return lax.conv_general_dilated(
        x, w[:, None, :].astype(x.dtype), window_strides=(1,), padding=[(K // 2, K // 2)],
        dimension_numbers=('NWC', 'WIO', 'NWC'), feature_group_count=C)


def l2_normalize(x):
    return x * lax.rsqrt(jnp.sum(x * x, axis=-1, keepdims=True) + NORM_EPS)


def gated_delta_chunked(q, k, v, g, beta, state):
    B, T, H, dk = q.shape
    dv = v.shape[-1]
    n, C = T // GDN_CHUNK, GDN_CHUNK
    q = q.reshape(B, n, C, H, dk).transpose(1, 0, 3, 2, 4)
    k = k.reshape(B, n, C, H, dk).transpose(1, 0, 3, 2, 4)
    v = v.reshape(B, n, C, H, dv).transpose(1, 0, 3, 2, 4)
    g = g.reshape(B, n, C, H).transpose(1, 0, 3, 2)
    beta = beta.reshape(B, n, C, H).transpose(1, 0, 3, 2)
    g_cum = jnp.cumsum(g, axis=-1)
    tril = jnp.tril(jnp.ones((C, C), dtype=bool))
    strict = jnp.tril(jnp.ones((C, C), dtype=jnp.float32), -1)
    decay = jnp.exp(jnp.where(tril, g_cum[..., :, None] - g_cum[..., None, :], -jnp.inf))
    kb = k * beta[..., None]
    vb = v * beta[..., None]
    lower = jnp.einsum('nbhid,nbhjd->nbhij', kb, k) * decay * strict
    a_mat = jnp.eye(C, dtype=jnp.float32) + lower
    u = lax.linalg.triangular_solve(a_mat, vb, left_side=True, lower=True, unit_diagonal=True)
    w = lax.linalg.triangular_solve(a_mat, kb * jnp.exp(g_cum)[..., None], left_side=True, lower=True,
                                    unit_diagonal=True)
    intra = jnp.einsum('nbhid,nbhjd->nbhij', q, k) * decay

    def step(s, xs):
        qi, ki, ui, wi, gi, ai = xs
        v_new = ui - jnp.einsum('bhcd,bhde->bhce', wi, s)
        o = jnp.einsum('bhcd,bhde->bhce', qi * jnp.exp(gi)[..., None], s) + jnp.einsum('bhij,bhje->bhie', ai, v_new)
        g_last = gi[..., -1]
        s = s * jnp.exp(g_last)[..., None, None] + jnp.einsum(
            'bhcd,bhce->bhde', ki * jnp.exp(g_last[..., None] - gi)[..., None], v_new)
        return s, o

    s_final, o = lax.scan(step, state, (q, k, u, w, g_cum, intra))
    return o.transpose(1, 0, 3, 2, 4).reshape(B, T, H, dv), s_final


def gdn_inputs(qkv, ab, conv_w, a_log, dt_bias):
    B, T, _ = qkv.shape
    qkv = jax.nn.silu(centred_depthwise_conv(qkv, conv_w)).astype(jnp.float32)
    qw = GDN_HEADS * GDN_K_DIM
    q = l2_normalize(qkv[..., :qw].reshape(B, T, GDN_HEADS, GDN_K_DIM)) * (GDN_K_DIM ** -0.5)
    k = l2_normalize(qkv[..., qw:2 * qw].reshape(B, T, GDN_HEADS, GDN_K_DIM))
    v = qkv[..., 2 * qw:].reshape(B, T, GDN_HEADS, GDN_V_DIM)
    ab = ab.astype(jnp.float32).reshape(B, T, 4, GDN_HEADS)
    g = -jnp.exp(a_log.astype(jnp.float32)) * jax.nn.softplus(ab[:, :, 0:2] + dt_bias.astype(jnp.float32))
    beta = jax.nn.sigmoid(ab[:, :, 2:4])
    return q, k, v, g, beta


def gated_deltanet(qkv_l, ab_l, qkv_c, ab_c, p):
    ql, kl, vl, gl, bl = gdn_inputs(qkv_l, ab_l, p['gdn_conv_w'], p['gdn_a_log'], p['gdn_dt_bias'])
    qc, kc, vc, gc, bc = gdn_inputs(qkv_c, ab_c, p['gdn_conv_w'], p['gdn_a_log'], p['gdn_dt_bias'])
    B = ql.shape[0]
    zero = jnp.zeros((B, GDN_HEADS, GDN_K_DIM, GDN_V_DIM), jnp.float32)
    flip = lambda a: jnp.flip(a, axis=1)
    oc_f, s_f = gated_delta_chunked(qc, kc, vc, gc[:, :, 0], bc[:, :, 0], zero)
    ol_f, _ = gated_delta_chunked(ql, kl, vl, gl[:, :, 0], bl[:, :, 0], s_f)
    oc_b, s_b = gated_delta_chunked(flip(qc), flip(kc), flip(vc), flip(gc[:, :, 1]), flip(bc[:, :, 1]), zero)
    ol_b, _ = gated_delta_chunked(flip(ql), flip(kl), flip(vl), flip(gl[:, :, 1]), flip(bl[:, :, 1]), s_b)
    return ol_f + flip(ol_b), oc_f + flip(oc_b)


def gdn_output(o, z, gain, dtype):
    B, T = z.shape[:2]
    y = rms_norm(o, gain) * jax.nn.silu(z.astype(jnp.float32).reshape(B, T, GDN_HEADS, GDN_V_DIM))
    return y.reshape(B, T, GDN_HEADS * GDN_V_DIM).astype(dtype)


def token_mixers(h_lat, h_ctx, p, layer_idx, rope_m, rope_d, with_ctx):
    B, T, _ = h_lat.shape
    Tc = h_ctx.shape[1]
    (cq_l, ckv_l, kr_l, gq_l, gk_l, gv_l, gz_l, gab_l, dq_l, dk_l, dv_l) = split_in(h_lat @ p['w_in'])
    (cq_c, ckv_c, kr_c, gq_c, gk_c, gv_c, gz_c, gab_c, dq_c, dk_c, dv_c) = split_in(h_ctx @ p['w_in'])

    q_l, k_l, v_l = mla_qkv(cq_l, ckv_l, kr_l, p, rope_m)
    q_c, k_c, v_c = mla_qkv(cq_c, ckv_c, kr_c, p, None)
    ones = jnp.ones((1,), jnp.float32)
    mla_l = block_attention(q_l, jnp.concatenate([k_l, k_c], axis=1), jnp.concatenate([v_l, v_c], axis=1),
                            ones, MLA_SCALE).reshape(B, T, MLA_HEADS * MLA_V_DIM)

    o_gl, o_gc = gated_deltanet(jnp.concatenate([gq_l, gk_l, gv_l], axis=-1), gab_l,
                                jnp.concatenate([gq_c, gk_c, gv_c], axis=-1), gab_c, p)
    gdn_l = gdn_output(o_gl, gz_l, p['gdn_norm_g'], h_lat.dtype)

    lam_p = p['diff_lambda'].astype(jnp.float32)
    lam_init = 0.8 - 0.6 * math.exp(-0.3 * layer_idx)
    lam = jnp.exp(jnp.sum(lam_p[0] * lam_p[1])) - jnp.exp(jnp.sum(lam_p[2] * lam_p[3])) + lam_init
    map_w = jnp.stack([jnp.ones_like(lam), -lam])
    dq_lt, dk_lt, dv_lt = diff_qkv(dq_l, dk_l, dv_l, rope_d)
    dq_ct, dk_ct, dv_ct = diff_qkv(dq_c, dk_c, dv_c, None)
    diff_l = block_attention(dq_lt, jnp.concatenate([dk_lt, dk_ct], axis=1), jnp.concatenate([dv_lt, dv_ct], axis=1),
                             map_w, DIFF_SCALE)
    diff_l = (rms_norm(diff_l, p['diff_norm_g']) * (1.0 - lam_init)).reshape(B, T, DIFF_HEADS * DIFF_V_DIM)

    out_l = jnp.concatenate([mla_l, gdn_l, diff_l], axis=-1) @ p['w_out']
    if not with_ctx:
        return out_l, None

    mla_c = block_attention(q_c, k_c, v_c, ones, MLA_SCALE).reshape(B, Tc, MLA_HEADS * MLA_V_DIM)
    gdn_c = gdn_output(o_gc, gz_c, p['gdn_norm_g'], h_ctx.dtype)
    diff_c = block_attention(dq_ct, dk_ct, dv_ct, map_w, DIFF_SCALE)
    diff_c = (rms_norm(diff_c, p['diff_norm_g']) * (1.0 - lam_init)).reshape(B, Tc, DIFF_HEADS * DIFF_V_DIM)
    out_c = jnp.concatenate([mla_c, gdn_c, diff_c], axis=-1) @ p['w_out']
    return out_l, out_c


def sq_relu_mlp(h, w1, w2):
    return jnp.square(jax.nn.relu(h @ w1)) @ w2


def setup_inputs(seed: int = 0) -> dict:
    key = jax.random.key(seed)
    ks = jax.random.split(key, 24)
    L, D = DEPTH, D_MODEL
    f32 = jnp.float32

    def nrm(k, shape, scale):
        return jax.random.normal(k, shape, f32) * scale

    def gain(k, shape):
        return 1.0 + 0.02 * jax.random.normal(k, shape, f32)

    conv_ch = 2 * GDN_HEADS * GDN_K_DIM + GDN_HEADS * GDN_V_DIM
    dt = jnp.exp(jax.random.uniform(ks[15], (L, 2, GDN_HEADS), f32, math.log(1e-3), math.log(1e-1)))
    return {
        'x': nrm(ks[0], (BATCH, SEQ, D), 1.0),
        'c': nrm(ks[1], (BATCH, D), 1.0),
        'ctx': nrm(ks[2], (BATCH, CTX_LEN, D), 1.0),
        'c_ctx': nrm(ks[3], (D,), 1.0),
        'w_ada': nrm(ks[4], (L, D, 6 * D), 0.5 * D ** -0.5),
        'b_ada': nrm(ks[5], (L, 6 * D), 0.02),
        'norm1_g': gain(ks[6], (L, D)),
        'norm2_g': gain(ks[7], (L, D)),
        'w_in': nrm(ks[8], (L, D, D_IN), D ** -0.5),
        'mla_q_norm_g': gain(ks[9], (L, MLA_Q_LORA)),
        'mla_kv_norm_g': gain(ks[10], (L, MLA_KV_LORA)),
        'mla_w_uq': nrm(ks[11], (L, MLA_Q_LORA, MLA_HEADS * (MLA_NOPE_DIM + MLA_ROPE_DIM)), MLA_Q_LORA ** -0.5),
        'mla_w_ukv': nrm(ks[12], (L, MLA_KV_LORA, MLA_HEADS * (MLA_NOPE_DIM + MLA_V_DIM)), MLA_KV_LORA ** -0.5),
        'gdn_conv_w': nrm(ks[13], (L, GDN_CONV, conv_ch), GDN_CONV ** -0.5),
        'gdn_a_log': jnp.log(jax.random.uniform(ks[14], (L, 2, GDN_HEADS), f32, 1.0, 16.0)),
        'gdn_dt_bias': dt + jnp.log(-jnp.expm1(-dt)),
        'gdn_norm_g': gain(ks[16], (L, GDN_V_DIM)),
        'diff_lambda': nrm(ks[17], (L, 4, DIFF_QK_DIM), 0.1),
        'diff_norm_g': gain(ks[18], (L, DIFF_V_DIM)),
        'w_out': nrm(ks[19], (L, D_MIX, D), D_MIX ** -0.5),
        'w_mlp1': nrm(ks[20], (L, D, D_FF), D ** -0.5),
        'w_mlp2': nrm(ks[21], (L, D_FF, D), D_FF ** -0.5),
        'final_norm_g': gain(ks[22], (D,)),
    }


def reference(x, c, ctx, c_ctx, w_ada, b_ada, norm1_g, norm2_g, w_in, mla_q_norm_g, mla_kv_norm_g,
              mla_w_uq, mla_w_ukv, gdn_conv_w, gdn_a_log, gdn_dt_bias, gdn_norm_g, diff_lambda, diff_norm_g,
              w_out, w_mlp1, w_mlp2, final_norm_g):
    n_lat = x.shape[1]
    rows = n_lat // GRID_W
    row_pos = jnp.repeat(jnp.arange(rows, dtype=jnp.float32), GRID_W)
    col_pos = jnp.tile(jnp.arange(GRID_W, dtype=jnp.float32), rows)
    rope_m = axial_rope_tables(row_pos, col_pos, MLA_ROPE_DIM)
    rope_d = axial_rope_tables(row_pos, col_pos, DIFF_QK_DIM)
    s_lat = jax.nn.silu(c)
    s_ctx = jax.nn.silu(c_ctx)
    for i in range(DEPTH):
        with_ctx = i < DEPTH - 1
        p = {
            'w_in': w_in[i], 'mla_q_norm_g': mla_q_norm_g[i], 'mla_kv_norm_g': mla_kv_norm_g[i],
            'mla_w_uq': mla_w_uq[i], 'mla_w_ukv': mla_w_ukv[i], 'gdn_conv_w': gdn_conv_w[i],
            'gdn_a_log': gdn_a_log[i], 'gdn_dt_bias': gdn_dt_bias[i], 'gdn_norm_g': gdn_norm_g[i],
            'diff_lambda': diff_lambda[i], 'diff_norm_g': diff_norm_g[i], 'w_out': w_out[i],
        }
        m_l = jnp.split((s_lat @ w_ada[i] + b_ada[i])[:, None, :], 6, axis=-1)
        m_c = jnp.split(s_ctx @ w_ada[i] + b_ada[i], 6, axis=-1)
        h_lat = modulate(rms_norm(x, norm1_g[i]), m_l[0], m_l[1])
        h_ctx = modulate(rms_norm(ctx, norm1_g[i]), m_c[0], m_c[1])
        mix_l, mix_c = token_mixers(h_lat, h_ctx, p, i, rope_m, rope_d, with_ctx)
        x = x + m_l[2] * mix_l
        x = x + m_l[5] * sq_relu_mlp(modulate(rms_norm(x, norm2_g[i]), m_l[3], m_l[4]), w_mlp1[i], w_mlp2[i])
        if with_ctx:
            ctx = ctx + m_c[2] * mix_c
            ctx = ctx + m_c[5] * sq_relu_mlp(modulate(rms_norm(ctx, norm2_g[i]), m_c[3], m_c[4]),
                                             w_mlp1[i], w_mlp2[i])
    return rms_norm(x, final_norm_g)
```

```python
import functools
import math

import jax
import jax.numpy as jnp
from jax import lax
from jax.experimental import pallas as pl
from jax.experimental.pallas import tpu as pltpu

F32 = jnp.float32
BF16 = jnp.bfloat16

GRID_W = 64
MLA_HEADS = 6
MLA_Q_LORA = 768
MLA_KV_LORA = 512
MLA_NOPE_DIM = 128
MLA_ROPE_DIM = 64
MLA_V_DIM = 128
MLA_SCALE = (MLA_NOPE_DIM + MLA_ROPE_DIM) ** -0.5
GDN_HEADS = 6
GDN_K_DIM = 128
GDN_V_DIM = 128
GDN_CONV = 5
GDN_CHUNK = 64
DIFF_HEADS = 4
DIFF_QK_DIM = 64
DIFF_V_DIM = 128
DIFF_SCALE = DIFF_QK_DIM ** -0.5
ROPE_BASE = 10000.0
NORM_EPS = 1e-6

LANES = 128
MLA_QK_PAD = 2 * LANES
VMEM_LIMIT = 56 * 1024 * 1024
NORM_ROWS = 16
GDN_TILE = 4 * GDN_CHUNK

GQKV_W = 2 * GDN_HEADS * GDN_K_DIM + GDN_HEADS * GDN_V_DIM
P_CQ = 0
P_GQKV = P_CQ + MLA_Q_LORA
P_GZ = P_GQKV + GQKV_W
P_KR = P_GZ + GDN_HEADS * GDN_V_DIM
P_GAB = P_KR + LANES
P_CKV = P_GAB + LANES
P_DQ = P_CKV + MLA_KV_LORA
P_DK = P_DQ + DIFF_HEADS * 2 * DIFF_QK_DIM
P_DV = P_DK + DIFF_HEADS * 2 * DIFF_QK_DIM
P_TOTAL = P_DV + DIFF_HEADS * DIFF_V_DIM


def _cparams(sem):
    return pltpu.CompilerParams(dimension_semantics=sem, vmem_limit_bytes=VMEM_LIMIT)


def _row_tile(s, cap):
    for step in (128, 64):
        best = 0
        for t in range(step, min(s, cap) + 1, step):
            if s % t == 0:
                best = t
        if best:
            return best
    raise ValueError(f"no row tile for {s}")


def _col_tile(n, cap):
    best = 0
    for t in range(LANES, min(n, cap) + 1, LANES):
        if n % t == 0:
            best = t
    return best


def _rms(x, g):
    ms = jnp.mean(x * x, axis=-1, keepdims=True)
    return x * lax.rsqrt(ms + NORM_EPS) * g


def _silu(x):
    return x * jax.nn.sigmoid(x)


def _norm_mod_rows(x_ref, g_ref, shl, scl, shc, scc, hn_ref, first_row, tc):
    tm = x_ref.shape[0]
    rc = NORM_ROWS
    g = g_ref[...]
    sh_l, sc_l, sh_c, sc_c = shl[...], scl[...], shc[...], scc[...]

    def body(r, carry):
        r0 = pl.multiple_of(r * rc, rc)
        is_ctx = (first_row + r0 + lax.broadcasted_iota(jnp.int32, (rc, 1), 0)) < tc
        y = _rms(x_ref[pl.ds(r0, rc), :], g)
        scale = jnp.where(is_ctx, sc_c, sc_l)
        shift = jnp.where(is_ctx, sh_c, sh_l)
        hn_ref[pl.ds(r0, rc), :] = (y * (1.0 + scale) + shift).astype(hn_ref.dtype)
        return carry

    lax.fori_loop(0, tm // rc, body, 0)


def _ctx_rows(i, tpb, tm, tc):
    rows = (i % tpb) * tm + lax.broadcasted_iota(jnp.int32, (tm, 1), 0)
    return rows < tc


def _rot_half(x, quarter):
    ax = x.ndim - 1
    n = x.shape[ax]
    lane = lax.broadcasted_iota(jnp.int32, x.shape, ax)
    nxt = pltpu.roll(x, n - quarter, ax)
    prv = pltpu.roll(x, quarter, ax)
    return jnp.where((lane % (2 * quarter)) < quarter, -nxt, prv)


def _adaln_kernel(s_ref, w_ref, b_ref, o_ref):
    s = _silu(s_ref[...])
    acc = jnp.dot(s.astype(BF16), w_ref[...].astype(BF16), preferred_element_type=F32)
    o_ref[...] = acc + b_ref[...]


def _adaln(s_rows, w_ada, b_ada):
    depth, d, n6 = w_ada.shape
    r = s_rows.shape[0]
    tn = _col_tile(n6, 1024)
    return pl.pallas_call(
        _adaln_kernel,
        grid=(depth, n6 // tn),
        in_specs=[
            pl.BlockSpec((r, d), lambda l, j: (0, 0)),
            pl.BlockSpec((None, d, tn), lambda l, j: (l, 0, j)),
            pl.BlockSpec((None, 1, tn), lambda l, j: (l, 0, j)),
        ],
        out_specs=pl.BlockSpec((None, r, tn), lambda l, j: (l, 0, j)),
        out_shape=jax.ShapeDtypeStruct((depth, r, n6), F32),
        compiler_params=_cparams(("arbitrary", "arbitrary")),
        name="adaln",
    )(s_rows, w_ada, b_ada.reshape(depth, 1, n6))


def _mod_specs(d, tpb, nb, ks):
    specs = []
    for k in ks:
        specs.append(pl.BlockSpec((None, None, 1, d), lambda i, j, k=k: (i // tpb, k, 0, 0)))
    for k in ks:
        specs.append(pl.BlockSpec((None, None, 1, d), lambda i, j, k=k: (nb, k, 0, 0)))
    return specs


def _in_proj_kernel(x_ref, g_ref, shl, scl, shc, scc, w_ref, o_ref, hn_ref, *, tm, tpb, tc):
    i = pl.program_id(0)

    @pl.when(pl.program_id(1) == 0)
    def _():
        _norm_mod_rows(x_ref, g_ref, shl, scl, shc, scc, hn_ref, (i % tpb) * tm, tc)

    o_ref[...] = jnp.dot(hn_ref[...], w_ref[...], preferred_element_type=F32)


def _in_proj(xa, gain, mod, w, *, s, tc, nb):
    n, d = xa.shape
    nout = w.shape[1]
    tm = _row_tile(s, 1152)
    tn = _col_tile(nout, 512)
    tpb = s // tm
    return pl.pallas_call(
        functools.partial(_in_proj_kernel, tm=tm, tpb=tpb, tc=tc),
        grid=(n // tm, nout // tn),
        in_specs=[
            pl.BlockSpec((tm, d), lambda i, j: (i, 0)),
            pl.BlockSpec((1, d), lambda i, j: (0, 0)),
            *_mod_specs(d, tpb, nb, (0, 1)),
            pl.BlockSpec((d, tn), lambda i, j: (0, j)),
        ],
        out_specs=pl.BlockSpec((tm, tn), lambda i, j: (i, j)),
        out_shape=jax.ShapeDtypeStruct((n, nout), F32),
        scratch_shapes=[pltpu.VMEM((tm, d), BF16)],
        compiler_params=_cparams(("arbitrary", "arbitrary")),
        name="in_proj",
    )(xa, gain, mod, mod, mod, mod, w)


def _mla_prep_kernel(cq_ref, ckv_ref, kr_ref, cos_ref, sin_ref, gq_ref, gkv_ref, wq_ref, wk_ref, wvt_ref,
                     q_ref, k_ref, vt_ref):
    cos = cos_ref[...]
    sin = sin_ref[...]
    quarter = MLA_ROPE_DIM // 4

    def rope(x):
        return x * cos + _rot_half(x, quarter) * sin

    qn = _rms(cq_ref[...], gq_ref[...]).astype(BF16)
    q = jnp.dot(qn, wq_ref[...], preferred_element_type=F32)
    kvn = _rms(ckv_ref[...], gkv_ref[...]).astype(BF16)
    kn = jnp.dot(kvn, wk_ref[...], preferred_element_type=F32)
    vt = lax.dot_general(wvt_ref[...], kvn, (((1,), (1,)), ((), ())), preferred_element_type=F32)
    vt_ref[...] = vt.astype(BF16)
    kr = rope(kr_ref[...]).astype(BF16)
    for h in range(MLA_HEADS):
        a = h * MLA_QK_PAD
        q_ref[:, a:a + LANES] = (q[:, a:a + LANES] * MLA_SCALE).astype(BF16)
        q_ref[:, a + LANES:a + 2 * LANES] = (rope(q[:, a + LANES:a + 2 * LANES]) * MLA_SCALE).astype(BF16)
        k_ref[:, a:a + LANES] = kn[:, h * LANES:(h + 1) * LANES].astype(BF16)
        k_ref[:, a + LANES:a + 2 * LANES] = kr


def _mla_prep(proj, cos_t, sin_t, g_q, g_kv, wq, wk, wvt, *, s):
    n = proj.shape[0]
    tm = _row_tile(s, 768)
    tpb = s // tm
    hq = MLA_HEADS * MLA_QK_PAD
    hv = MLA_HEADS * MLA_V_DIM
    full = lambda a: pl.BlockSpec(a.shape, lambda i: (0,) * a.ndim)
    return pl.pallas_call(
        _mla_prep_kernel,
        grid=(n // tm,),
        in_specs=[
            pl.BlockSpec((tm, MLA_Q_LORA), lambda i: (i, P_CQ // MLA_Q_LORA)),
            pl.BlockSpec((tm, MLA_KV_LORA), lambda i: (i, P_CKV // MLA_KV_LORA)),
            pl.BlockSpec((tm, LANES), lambda i: (i, P_KR // LANES)),
            pl.BlockSpec((tm, LANES), lambda i: (i % tpb, 0)),
            pl.BlockSpec((tm, LANES), lambda i: (i % tpb, 0)),
            full(g_q), full(g_kv), full(wq), full(wk), full(wvt),
        ],
        out_specs=[
            pl.BlockSpec((tm, hq), lambda i: (i, 0)),
            pl.BlockSpec((tm, hq), lambda i: (i, 0)),
            pl.BlockSpec((hv, tm), lambda i: (0, i)),
        ],
        out_shape=[
            jax.ShapeDtypeStruct((n, hq), BF16),
            jax.ShapeDtypeStruct((n, hq), BF16),
            jax.ShapeDtypeStruct((hv, n), BF16),
        ],
        compiler_params=_cparams(("arbitrary",)),
        name="mla_prep",
    )(proj, proj, proj, cos_t, sin_t, g_q, g_kv, wq, wk, wvt)


def _softplus(x):
    return jnp.maximum(x, 0.0) + jnp.log1p(jnp.exp(-jnp.abs(x)))


def _aux_prep_kernel(dq_ref, dk_ref, dv_ref, ab_ref, cos_ref, sin_ref, alog_ref, dtb_ref,
                     q_ref, k_ref, vt_ref, gates_ref, gt_ref, *, tm):
    cos = cos_ref[...]
    sin = sin_ref[...]
    quarter = DIFF_QK_DIM // 4
    for h in range(DIFF_HEADS):
        sl = slice(h * LANES, (h + 1) * LANES)
        xq = dq_ref[:, sl]
        xk = dk_ref[:, sl]
        q_ref[:, sl] = ((xq * cos + _rot_half(xq, quarter) * sin) * DIFF_SCALE).astype(BF16)
        k_ref[:, sl] = (xk * cos + _rot_half(xk, quarter) * sin).astype(BF16)
    vt_ref[...] = dv_ref[...].T.astype(BF16)

    ab = ab_ref[...]
    g = -jnp.exp(alog_ref[...]) * _softplus(ab + dtb_ref[...])
    beta = jax.nn.sigmoid(ab)
    row = lax.broadcasted_iota(jnp.int32, (tm, LANES), 0) % GDN_CHUNK
    pre = g
    suf = g
    step = 1
    while step < GDN_CHUNK:
        pre = pre + jnp.where(row >= step, pltpu.roll(pre, step, 0), 0.0)
        suf = suf + jnp.where(row < GDN_CHUNK - step, pltpu.roll(suf, tm - step, 0), 0.0)
        step *= 2
    lane = lax.broadcasted_iota(jnp.int32, (tm, LANES), 1)
    total = pltpu.roll(pre + suf - g, 4 * GDN_HEADS, 1)
    gates = jnp.where(lane < GDN_HEADS, pre,
                      jnp.where(lane < 2 * GDN_HEADS, suf, jnp.where(lane < 4 * GDN_HEADS, beta, total)))
    gates_ref[...] = gates
    gt_ref[...] = gates.T[0:gt_ref.shape[0], :]


def _aux_prep(proj, cos_t, sin_t, alog_row, dtb_row, *, s):
    n = proj.shape[0]
    tm = _row_tile(s, 768)
    tpb = s // tm
    wq = DIFF_HEADS * 2 * DIFF_QK_DIM
    wv = DIFF_HEADS * DIFF_V_DIM
    gt_rows = 32
    return pl.pallas_call(
        functools.partial(_aux_prep_kernel, tm=tm),
        grid=(n // tm,),
        in_specs=[
            pl.BlockSpec((tm, wq), lambda i: (i, P_DQ // wq)),
            pl.BlockSpec((tm, wq), lambda i: (i, P_DK // wq)),
            pl.BlockSpec((tm, wv), lambda i: (i, P_DV // wv)),
            pl.BlockSpec((tm, LANES), lambda i: (i, P_GAB // LANES)),
            pl.BlockSpec((tm, LANES), lambda i: (i % tpb, 0)),
            pl.BlockSpec((tm, LANES), lambda i: (i % tpb, 0)),
            pl.BlockSpec((1, LANES), lambda i: (0, 0)),
            pl.BlockSpec((1, LANES), lambda i: (0, 0)),
        ],
        out_specs=[
            pl.BlockSpec((tm, wq), lambda i: (i, 0)),
            pl.BlockSpec((tm, wq), lambda i: (i, 0)),
            pl.BlockSpec((wv, tm), lambda i: (0, i)),
            pl.BlockSpec((tm, LANES), lambda i: (i, 0)),
            pl.BlockSpec((gt_rows, tm), lambda i: (0, i)),
        ],
        out_shape=[
            jax.ShapeDtypeStruct((n, wq), BF16),
            jax.ShapeDtypeStruct((n, wq), BF16),
            jax.ShapeDtypeStruct((wv, n), BF16),
            jax.ShapeDtypeStruct((n, LANES), F32),
            jax.ShapeDtypeStruct((gt_rows, n), F32),
        ],
        compiler_params=_cparams(("arbitrary",)),
        name="aux_prep",
    )(proj, proj, proj, proj, cos_t, sin_t, alog_row, dtb_row)


def _gdn_prep_kernel(x_ref, w_ref, o_ref, *, s, tc, groups):
    c = pl.program_id(1)
    x = x_ref[...]
    w = w_ref[...]
    t = lax.broadcasted_iota(jnp.int32, (s, 1), 0)
    lo = jnp.where(t < tc, 0, tc)
    hi = jnp.where(t < tc, tc, s)
    half = GDN_CONV // 2
    acc = x * w[half:half + 1, :]
    for off in range(-half, half + 1):
        if off == 0:
            continue
        xs = pltpu.roll(x, (-off) % s, 0)
        ok = (t + off >= lo) & (t + off < hi)
        acc = acc + jnp.where(ok, xs, 0.0) * w[off + half:off + half + 1, :]
    y = _silu(acc)
    qscale = jnp.where(c < groups, GDN_K_DIM ** -0.5, 1.0)
    is_qk = c < 2 * groups
    for gi in range(x.shape[1] // LANES):
        sl = slice(gi * LANES, (gi + 1) * LANES)
        yg = y[:, sl]
        nrm = yg * lax.rsqrt(jnp.sum(yg * yg, axis=-1, keepdims=True) + NORM_EPS) * qscale
        o_ref[:, sl] = jnp.where(is_qk, nrm, yg)


def _gdn_prep(proj, conv_w, *, s, tc, nb):
    n = proj.shape[0]
    cw = 3 * LANES
    groups = GDN_HEADS * GDN_K_DIM // cw
    off = P_GQKV // cw
    return pl.pallas_call(
        functools.partial(_gdn_prep_kernel, s=s, tc=tc, groups=groups),
        grid=(nb, GQKV_W // cw),
        in_specs=[
            pl.BlockSpec((s, cw), lambda b, c: (b, off + c)),
            pl.BlockSpec((GDN_CONV, cw), lambda b, c: (0, c)),
        ],
        out_specs=pl.BlockSpec((s, cw), lambda b, c: (b, c)),
        out_shape=jax.ShapeDtypeStruct((n, GQKV_W), F32),
        compiler_params=_cparams(("arbitrary", "arbitrary")),
        name="gdn_prep",
    )(proj, conv_w)


def _gdn_solve_kernel(q_ref, k_ref, v_ref, gates_ref, gt_ref, u_ref, w_ref, qg_ref, kd_ref, in_ref):
    t = q_ref.shape[0]
    c = GDN_CHUNK
    nh = GDN_HEADS
    ii = lax.broadcasted_iota(jnp.int32, (t, t), 0)
    jj = lax.broadcasted_iota(jnp.int32, (t, t), 1)
    log2c = c.bit_length() - 1
    same = (ii >> log2c) == (jj >> log2c)
    eye = jnp.where(ii == jj, 1.0, 0.0)
    level_masks = [((ii >> (sh + 1)) == (jj >> (sh + 1))) & ((ii >> sh) != (jj >> sh))
                   for sh in range(1, log2c)]
    pair = (ii >> 1) == (jj >> 1)
    gates = gates_ref[...]
    gt = gt_ref[...]
    nt = (((1,), (1,)), ((), ()))
    for h in range(nh):
        sl = slice(h * LANES, (h + 1) * LANES)
        q = q_ref[:, sl]
        k = k_ref[:, sl]
        v = v_ref[:, sl]
        kb = k.astype(BF16)
        kk = lax.dot_general(kb, kb, nt, preferred_element_type=F32)
        qk = lax.dot_general(q.astype(BF16), kb, nt, preferred_element_type=F32)
        for d in range(2):
            fwd = d == 0
            gi = d * nh + h
            gcol = gates[:, gi:gi + 1]
            bcol = gates[:, 2 * nh + gi:2 * nh + gi + 1]
            glcol = gates[:, 4 * nh + gi:4 * nh + gi + 1]
            grow = gt[gi:gi + 1, :]
            incl = same & ((ii >= jj) if fwd else (ii <= jj))
            strict = same & ((ii > jj) if fwd else (ii < jj))
            decay = jnp.where(incl, jnp.exp(jnp.where(incl, gcol - grow, 0.0)), 0.0)
            lmat = jnp.where(strict, bcol * kk * decay, 0.0)
            x = eye - jnp.where(pair, lmat, 0.0)
            for m in level_masks:
                xb = x.astype(BF16)
                nx = jnp.dot(jnp.where(m, lmat, 0.0).astype(BF16), xb, preferred_element_type=F32)
                x = x - jnp.dot(xb, nx.astype(BF16), preferred_element_type=F32)
            eg = jnp.exp(gcol)
            rhs = jnp.concatenate([v * bcol, k * (bcol * eg)], axis=1).astype(BF16)
            uw = jnp.dot(x.astype(BF16), rhs, preferred_element_type=F32)
            cs = slice(gi * LANES, (gi + 1) * LANES)
            u_ref[:, cs] = uw[:, :GDN_V_DIM]
            w_ref[:, cs] = uw[:, GDN_V_DIM:].astype(BF16)
            qg_ref[:, cs] = (q * eg).astype(BF16)
            kd_ref[:, cs] = (k * jnp.exp(glcol - gcol)).astype(BF16)
            intra = qk * decay
            compact = jnp.concatenate([intra[n * c:(n + 1) * c, n * c:(n + 1) * c] for n in range(t // c)],
                                      axis=0)
            in_ref[:, cs] = jnp.concatenate([compact, jnp.zeros((t, LANES - c), F32)], axis=1).astype(BF16)


def _gdn_solve(qkv, gates, gt, *, s, nb):
    n = qkv.shape[0]
    t = GDN_TILE
    nt = s // t
    hw = GDN_HEADS * GDN_K_DIM
    cw = 2 * hw
    row = lambda b, i: (b * nt + i, 0)
    return pl.pallas_call(
        _gdn_solve_kernel,
        grid=(nb, nt),
        in_specs=[
            pl.BlockSpec((t, hw), lambda b, i: (b * nt + i, 0)),
            pl.BlockSpec((t, hw), lambda b, i: (b * nt + i, 1)),
            pl.BlockSpec((t, hw), lambda b, i: (b * nt + i, 2)),
            pl.BlockSpec((t, LANES), row),
            pl.BlockSpec((gt.shape[0], t), lambda b, i: (0, b * nt + i)),
        ],
        out_specs=[pl.BlockSpec((t, cw), row)] * 5,
        out_shape=[jax.ShapeDtypeStruct((n, cw), F32)] + [jax.ShapeDtypeStruct((n, cw), BF16)] * 4,
        compiler_params=_cparams(("arbitrary", "arbitrary")),
        name="gdn_solve",
    )(qkv, qkv, qkv, gates, gt)


def _gdn_scan_kernel(uf, ub, wf, wb, qgf, qgb, kdf, kdb, inf, inb, gf_ref, gb_ref, of_ref, ob_ref, s_ref):
    @pl.when(pl.program_id(1) == 0)
    def _():
        s_ref[...] = jnp.zeros_like(s_ref)

    c = GDN_CHUNK
    nh = GDN_HEADS
    npt = uf.shape[0] // c
    tn = (((0,), (0,)), ((), ()))
    dirs = ((uf, wf, qgf, kdf, inf, gf_ref, of_ref), (ub, wb, qgb, kdb, inb, gb_ref, ob_ref))
    for p in range(npt):
        for d, (u_ref, w_ref, qg_ref, kd_ref, in_ref, g_ref, o_ref) in enumerate(dirs):
            pc = p if d == 0 else npt - 1 - p
            rs = slice(pc * c, (pc + 1) * c)
            totals = g_ref[pc * c:pc * c + 1, :]
            for h in range(nh):
                j = d * nh + h
                cs = slice(h * LANES, (h + 1) * LANES)
                st = s_ref[j]
                sb = st.astype(BF16)
                wq = jnp.concatenate([w_ref[rs, cs], qg_ref[rs, cs]], axis=0)
                r = jnp.dot(wq, sb, preferred_element_type=F32)
                v_new = (u_ref[rs, cs] - r[:c]).astype(BF16)
                intra = in_ref[rs, cs][:, :c]
                o_ref[rs, cs] = r[c:] + jnp.dot(intra, v_new, preferred_element_type=F32)
                decay = jnp.exp(totals[:, 4 * nh + j:4 * nh + j + 1])
                s_ref[j] = st * decay + lax.dot_general(kd_ref[rs, cs], v_new, tn,
                                                        preferred_element_type=F32)


def _gdn_scan(u, w, qg, kd, intra, gates, *, s, tc, nb):
    n = u.shape[0]
    t = GDN_TILE
    nt = s // t
    nct = tc // t
    hw = GDN_HEADS * GDN_V_DIM

    def bwd_tile(i):
        return jnp.where(i < nct, nct - 1 - i, nct + nt - 1 - i)

    fwd = lambda b, i: (b * nt + i, 0)
    bwd = lambda b, i: (b * nt + bwd_tile(i), 1)
    pair = [pl.BlockSpec((t, hw), fwd), pl.BlockSpec((t, hw), bwd)]
    return pl.pallas_call(
        _gdn_scan_kernel,
        grid=(nb, nt),
        in_specs=pair * 5 + [
            pl.BlockSpec((t, LANES), fwd),
            pl.BlockSpec((t, LANES), lambda b, i: (b * nt + bwd_tile(i), 0)),
        ],
        out_specs=[
            pl.BlockSpec((t, hw), fwd),
            pl.BlockSpec((t, hw), lambda b, i: (b * nt + bwd_tile(i), 0)),
        ],
        out_shape=[jax.ShapeDtypeStruct((n, hw), F32)] * 2,
        scratch_shapes=[pltpu.VMEM((2 * GDN_HEADS, GDN_K_DIM, GDN_V_DIM), F32)],
        compiler_params=_cparams(("arbitrary", "arbitrary")),
        name="gdn_scan",
    )(u, u, w, w, qg, qg, kd, kd, intra, intra, gates, gates)


def _gdn_out_kernel(of_ref, ob_ref, z_ref, gain_ref, o_ref):
    gain = gain_ref[...]
    for h in range(GDN_HEADS):
        sl = slice(h * LANES, (h + 1) * LANES)
        o = of_ref[:, sl] + ob_ref[:, sl]
        o_ref[:, sl] = (_rms(o, gain) * _silu(z_ref[:, sl])).astype(o_ref.dtype)


def _gdn_out(o_f, o_b, proj, gain, *, s):
    n, hw = o_f.shape
    tm = _row_tile(s, 768)
    return pl.pallas_call(
        _gdn_out_kernel,
        grid=(n // tm,),
        in_specs=[
            pl.BlockSpec((tm, hw), lambda i: (i, 0)),
            pl.BlockSpec((tm, hw), lambda i: (i, 0)),
            pl.BlockSpec((tm, hw), lambda i: (i, P_GZ // hw)),
            pl.BlockSpec((1, LANES), lambda i: (0, 0)),
        ],
        out_specs=pl.BlockSpec((tm, hw), lambda i: (i, 0)),
        out_shape=jax.ShapeDtypeStruct((n, hw), BF16),
        compiler_params=_cparams(("arbitrary",)),
        name="gdn_out",
    )(o_f, o_b, proj, gain)


def _softmax_t(st):
    m = jnp.max(st, axis=0, keepdims=True)
    e = jnp.exp(st - m)
    return e, 1.0 / jnp.sum(e, axis=0, keepdims=True)


def _mla_attn_kernel(q_ref, k_ref, vt_ref, o_ref, *, tc, n_ctx_tiles):
    qi = pl.program_id(2)
    nt = (((1,), (1,)), ((), ()))

    def attend(k, vt):
        st = lax.dot_general(k, q_ref[...], nt, preferred_element_type=F32)
        e, rinv = _softmax_t(st)
        ot = jnp.dot(vt, e.astype(BF16), preferred_element_type=F32) * rinv
        o_ref[...] = ot.T.astype(o_ref.dtype)

    @pl.when(qi < n_ctx_tiles)
    def _():
        attend(k_ref[0:tc, :], vt_ref[:, 0:tc])

    @pl.when(qi >= n_ctx_tiles)
    def _():
        attend(k_ref[...], vt_ref[...])


def _mla_attn(q, k, vt, *, s, tc, nb):
    n = q.shape[0]
    tq = _row_tile(tc, 256)
    nq = s // tq
    return pl.pallas_call(
        functools.partial(_mla_attn_kernel, tc=tc, n_ctx_tiles=tc // tq),
        grid=(nb, MLA_HEADS, nq),
        in_specs=[
            pl.BlockSpec((tq, MLA_QK_PAD), lambda b, h, i: (b * nq + i, h)),
            pl.BlockSpec((s, MLA_QK_PAD), lambda b, h, i: (b, h)),
            pl.BlockSpec((MLA_V_DIM, s), lambda b, h, i: (h, b)),
        ],
        out_specs=pl.BlockSpec((tq, MLA_V_DIM), lambda b, h, i: (b * nq + i, h)),
        out_shape=jax.ShapeDtypeStruct((n, MLA_HEADS * MLA_V_DIM), BF16),
        compiler_params=_cparams(("arbitrary", "arbitrary", "arbitrary")),
        name="mla_attn",
    )(q, k, vt)


def _diff_attn_kernel(q_ref, k_ref, vt_ref, lam_ref, gain_ref, o_ref, *, tc, n_ctx_tiles, lam_init):
    qi = pl.program_id(2)
    nt = (((1,), (1,)), ((), ()))
    lp = lam_ref[...]
    lam = (jnp.exp(jnp.sum(lp[0:1] * lp[1:2], axis=-1, keepdims=True))
           - jnp.exp(jnp.sum(lp[2:3] * lp[3:4], axis=-1, keepdims=True)) + lam_init)

    def attend(k, vt):
        q = q_ref[...]
        lane = lax.broadcasted_iota(jnp.int32, q.shape, 1)
        zero = jnp.zeros_like(q)
        q1 = jnp.where(lane < DIFF_QK_DIM, q, zero)
        q2 = jnp.where(lane < DIFF_QK_DIM, zero, q)
        e1, r1 = _softmax_t(lax.dot_general(k, q1, nt, preferred_element_type=F32))
        e2, r2 = _softmax_t(lax.dot_general(k, q2, nt, preferred_element_type=F32))
        p = e1 * r1 - e2 * (r2 * lam)
        ot = jnp.dot(vt, p.astype(BF16), preferred_element_type=F32)
        ms = jnp.mean(ot * ot, axis=0, keepdims=True)
        ot = ot * lax.rsqrt(ms + NORM_EPS) * gain_ref[...] * (1.0 - lam_init)
        o_ref[...] = ot.T.astype(o_ref.dtype)

    @pl.when(qi < n_ctx_tiles)
    def _():
        attend(k_ref[0:tc, :], vt_ref[:, 0:tc])

    @pl.when(qi >= n_ctx_tiles)
    def _():
        attend(k_ref[...], vt_ref[...])


def _diff_attn(q, k, vt, lam_p, gain_col, *, s, tc, nb, lam_init):
    n = q.shape[0]
    tq = _row_tile(tc, 256)
    nq = s // tq
    return pl.pallas_call(
        functools.partial(_diff_attn_kernel, tc=tc, n_ctx_tiles=tc // tq, lam_init=lam_init),
        grid=(nb, DIFF_HEADS, nq),
        in_specs=[
            pl.BlockSpec((tq, LANES), lambda b, h, i: (b * nq + i, h)),
            pl.BlockSpec((s, LANES), lambda b, h, i: (b, h)),
            pl.BlockSpec((DIFF_V_DIM, s), lambda b, h, i: (h, b)),
            pl.BlockSpec(lam_p.shape, lambda b, h, i: (0, 0)),
            pl.BlockSpec(gain_col.shape, lambda b, h, i: (0, 0)),
        ],
        out_specs=pl.BlockSpec((tq, DIFF_V_DIM), lambda b, h, i: (b * nq + i, h)),
        out_shape=jax.ShapeDtypeStruct((n, DIFF_HEADS * DIFF_V_DIM), BF16),
        compiler_params=_cparams(("arbitrary", "arbitrary", "arbitrary")),
        name="diff_attn",
    )(q, k, vt, lam_p, gain_col)


def _out_proj_kernel(x_ref, a1_ref, a2_ref, a3_ref, w1_ref, w2_ref, w3_ref, gl_ref, gc_ref, o_ref,
                     *, tm, tpb, tc):
    i = pl.program_id(0)
    acc = jnp.dot(a1_ref[...], w1_ref[...], preferred_element_type=F32)
    acc += jnp.dot(a2_ref[...], w2_ref[...], preferred_element_type=F32)
    acc += jnp.dot(a3_ref[...], w3_ref[...], preferred_element_type=F32)
    gate = jnp.where(_ctx_rows(i, tpb, tm, tc), gc_ref[...], gl_ref[...])
    o_ref[...] = x_ref[...] + gate * acc


def _out_proj(xa, a1, a2, a3, w, mod, *, s, tc, nb):
    n, d = xa.shape
    tm = _row_tile(s, 1152)
    tn = _col_tile(d, 512)
    tpb = s // tm
    k1, k2, k3 = a1.shape[1], a2.shape[1], a3.shape[1]
    assert k1 == k2 and (k1 + k2) % k3 == 0
    return pl.pallas_call(
        functools.partial(_out_proj_kernel, tm=tm, tpb=tpb, tc=tc),
        grid=(n // tm, d // tn),
        in_specs=[
            pl.BlockSpec((tm, tn), lambda i, j: (i, j)),
            pl.BlockSpec((tm, k1), lambda i, j: (i, 0)),
            pl.BlockSpec((tm, k2), lambda i, j: (i, 0)),
            pl.BlockSpec((tm, k3), lambda i, j: (i, 0)),
            pl.BlockSpec((k1, tn), lambda i, j: (0, j)),
            pl.BlockSpec((k2, tn), lambda i, j: (1, j)),
            pl.BlockSpec((k3, tn), lambda i, j: ((k1 + k2) // k3, j)),
            pl.BlockSpec((None, None, 1, tn), lambda i, j: (i // tpb, 2, 0, j)),
            pl.BlockSpec((None, None, 1, tn), lambda i, j: (nb, 2, 0, j)),
        ],
        out_specs=pl.BlockSpec((tm, tn), lambda i, j: (i, j)),
        out_shape=jax.ShapeDtypeStruct((n, d), F32),
        compiler_params=_cparams(("arbitrary", "arbitrary")),
        name="out_proj",
    )(xa, a1, a2, a3, w, w, w, mod, mod)


def _mlp_kernel(x_ref, g_ref, shl, scl, shc, scc, gl_ref, gc_ref, w1_ref, w2_ref, o_ref, hn_ref,
                *, tm, tpb, tc):
    i = pl.program_id(0)
    j = pl.program_id(1)
    is_ctx = _ctx_rows(i, tpb, tm, tc)

    @pl.when(j == 0)
    def _():
        _norm_mod_rows(x_ref, g_ref, shl, scl, shc, scc, hn_ref, (i % tpb) * tm, tc)

    a = jnp.dot(hn_ref[...], w1_ref[...], preferred_element_type=F32)
    a = jnp.square(jnp.maximum(a, 0.0)).astype(BF16)
    part = jnp.dot(a, w2_ref[...], preferred_element_type=F32)

    @pl.when(j == 0)
    def _():
        o_ref[...] = part

    @pl.when(j > 0)
    def _():
        o_ref[...] += part

    @pl.when(j == pl.num_programs(1) - 1)
    def _():
        gate = jnp.where(is_ctx, gc_ref[...], gl_ref[...])
        o_ref[...] = x_ref[...] + gate * o_ref[...]


def _mlp(xa, gain, mod, w1, w2, *, s, tc, nb, n_rows):
    d = xa.shape[1]
    dff = w1.shape[1]
    tm = _row_tile(s, 768)
    tf = _col_tile(dff, 512)
    tpb = s // tm
    mod_gate = lambda row: pl.BlockSpec((None, None, 1, d), row)
    return pl.pallas_call(
        functools.partial(_mlp_kernel, tm=tm, tpb=tpb, tc=tc),
        grid=(n_rows // tm, dff // tf),
        in_specs=[
            pl.BlockSpec((tm, d), lambda i, j: (i, 0)),
            pl.BlockSpec((1, d), lambda i, j: (0, 0)),
            *_mod_specs(d, tpb, nb, (3, 4)),
            mod_gate(lambda i, j: (i // tpb, 5, 0, 0)),
            mod_gate(lambda i, j: (nb, 5, 0, 0)),
            pl.BlockSpec((d, tf), lambda i, j: (0, j)),
            pl.BlockSpec((tf, d), lambda i, j: (j, 0)),
        ],
        out_specs=pl.BlockSpec((tm, d), lambda i, j: (i, 0)),
        out_shape=jax.ShapeDtypeStruct((n_rows, d), F32),
        scratch_shapes=[pltpu.VMEM((tm, d), BF16)],
        compiler_params=_cparams(("arbitrary", "arbitrary")),
        name="mlp",
    )(xa, gain, mod, mod, mod, mod, mod, mod, w1, w2)


def _final_norm_kernel(x_ref, g_ref, o_ref):
    o_ref[...] = _rms(x_ref[...], g_ref[...])


def _final_norm(xa, gain, *, s, tc, nb):
    d = xa.shape[1]
    t = s - tc
    tr = _row_tile(math.gcd(tc, t), 512)
    per_b = s // tr
    skip = tc // tr
    nt = t // tr
    return pl.pallas_call(
        _final_norm_kernel,
        grid=(nb, nt),
        in_specs=[
            pl.BlockSpec((tr, d), lambda b, r: (b * per_b + skip + r, 0)),
            pl.BlockSpec((1, d), lambda b, r: (0, 0)),
        ],
        out_specs=pl.BlockSpec((tr, d), lambda b, r: (b * nt + r, 0)),
        out_shape=jax.ShapeDtypeStruct((nb * t, d), F32),
        compiler_params=_cparams(("arbitrary", "arbitrary")),
        name="final_norm",
    )(xa, gain)


def _rope_tables(t, tc, dim, reps):
    rows = t // GRID_W
    row_pos = jnp.repeat(jnp.arange(rows, dtype=F32), GRID_W)
    col_pos = jnp.tile(jnp.arange(GRID_W, dtype=F32), rows)
    quarter = dim // 4
    inv_freq = ROPE_BASE ** (-jnp.arange(quarter, dtype=F32) / quarter)
    ang_r = row_pos[:, None] * inv_freq[None, :]
    ang_c = col_pos[:, None] * inv_freq[None, :]
    ang = jnp.concatenate([ang_r, ang_r, ang_c, ang_c], axis=-1)
    cos = jnp.tile(jnp.cos(ang), (1, reps))
    sin = jnp.tile(jnp.sin(ang), (1, reps))
    pad = LANES - cos.shape[1]
    cos = jnp.pad(cos, ((0, 0), (0, pad)), constant_values=1.0)
    sin = jnp.pad(sin, ((0, 0), (0, pad)))
    cos = jnp.concatenate([jnp.ones((tc, LANES), F32), cos], axis=0)
    sin = jnp.concatenate([jnp.zeros((tc, LANES), F32), sin], axis=0)
    return cos, sin


def _permute_w_in(w):
    d = w.shape[0]
    sizes = (MLA_Q_LORA, MLA_KV_LORA, MLA_ROPE_DIM, GDN_HEADS * GDN_K_DIM, GDN_HEADS * GDN_K_DIM,
             GDN_HEADS * GDN_V_DIM, GDN_HEADS * GDN_V_DIM, 4 * GDN_HEADS,
             DIFF_HEADS * 2 * DIFF_QK_DIM, DIFF_HEADS * 2 * DIFF_QK_DIM, DIFF_HEADS * DIFF_V_DIM)
    offs = [0]
    for sz in sizes:
        offs.append(offs[-1] + sz)
    cq, ckv, kr, gq, gk, gv, gz, gab, dq, dk, dv = (w[:, offs[i]:offs[i + 1]] for i in range(len(sizes)))
    zpad = lambda a: jnp.pad(a, ((0, 0), (0, LANES - a.shape[1])))
    out = jnp.concatenate([cq, gq, gk, gv, gz, zpad(kr), zpad(gab), ckv, dq, dk, dv], axis=1)
    assert out.shape == (d, P_TOTAL)
    return out.astype(BF16)


def kernel(x, c, ctx, c_ctx, w_ada, b_ada, norm1_g, norm2_g, w_in, mla_q_norm_g, mla_kv_norm_g, mla_w_uq, mla_w_ukv, gdn_conv_w, gdn_a_log, gdn_dt_bias, gdn_norm_g, diff_lambda, diff_norm_g, w_out, w_mlp1, w_mlp2, final_norm_g):
    nb, t, d = x.shape
    tc = ctx.shape[1]
    depth = w_ada.shape[0]
    s = tc + t
    n = nb * s
    assert t % GRID_W == 0 and t % GDN_CHUNK == 0 and tc % GDN_CHUNK == 0

    xa = jnp.concatenate([ctx, x], axis=1).reshape(n, d)

    r = -(-(nb + 1) // 8) * 8
    s_rows = jnp.concatenate([c, c_ctx[None, :], jnp.zeros((r - nb - 1, d), F32)], axis=0)
    mod_all = _adaln(s_rows, w_ada, b_ada).reshape(depth, r, 6, 1, d)

    cos_m, sin_m = _rope_tables(t, tc, MLA_ROPE_DIM, 1)
    cos_d, sin_d = _rope_tables(t, tc, DIFF_QK_DIM, 2)
    hb = 1

    for l in range(depth):
        with_ctx = l < depth - 1
        mod = mod_all[l]
        lam_init = 0.8 - 0.6 * math.exp(-0.3 * l)

        w_in_p = _permute_w_in(w_in[l])
        wq = mla_w_uq[l].reshape(MLA_Q_LORA, MLA_HEADS, MLA_NOPE_DIM + MLA_ROPE_DIM)
        wq = jnp.pad(wq, ((0, 0), (0, 0), (0, MLA_QK_PAD - wq.shape[2])))
        wq = wq.reshape(MLA_Q_LORA, MLA_HEADS * MLA_QK_PAD).astype(BF16)
        wkv = mla_w_ukv[l].reshape(MLA_KV_LORA, MLA_HEADS, MLA_NOPE_DIM + MLA_V_DIM)
        wk = wkv[:, :, :MLA_NOPE_DIM].reshape(MLA_KV_LORA, MLA_HEADS * MLA_NOPE_DIM).astype(BF16)
        wvt = wkv[:, :, MLA_NOPE_DIM:].reshape(MLA_KV_LORA, MLA_HEADS * MLA_V_DIM).T.astype(BF16)
        gate_pad = LANES - 2 * GDN_HEADS
        alog_row = jnp.pad(gdn_a_log[l].reshape(1, 2 * GDN_HEADS), ((0, 0), (0, gate_pad)))
        dtb_row = jnp.pad(gdn_dt_bias[l].reshape(1, 2 * GDN_HEADS), ((0, 0), (0, gate_pad)))

        proj = _in_proj(xa, norm1_g[l][None, :], mod, w_in_p, s=s, tc=tc, nb=nb)

        mq, mk, mvt = _mla_prep(proj, cos_m, sin_m, mla_q_norm_g[l][None, :], mla_kv_norm_g[l][None, :],
                                wq, wk, wvt, s=s)
        mla_o = _mla_attn(mq, mk, mvt, s=s, tc=tc, nb=nb)

        dq, dk, dvt, gates, gt = _aux_prep(proj, cos_d, sin_d, alog_row, dtb_row, s=s)
        diff_o = _diff_attn(dq, dk, dvt, diff_lambda[l], diff_norm_g[l][:, None], s=s, tc=tc, nb=nb,
                            lam_init=lam_init)

        qkv = _gdn_prep(proj, gdn_conv_w[l], s=s, tc=tc, nb=nb)
        gu, gw, gqg, gkd, gin = _gdn_solve(qkv, gates, gt, s=s, nb=nb)
        o_f, o_b = _gdn_scan(gu, gw, gqg, gkd, gin, gates, s=s, tc=tc, nb=nb)
        gdn_o = _gdn_out(o_f, o_b, proj, gdn_norm_g[l][None, :], s=s)

        xa = _out_proj(xa, mla_o, gdn_o, diff_o, w_out[l].astype(BF16), mod, s=s, tc=tc, nb=nb)
        xa = _mlp(xa, norm2_g[l][None, :], mod, w_mlp1[l].astype(BF16), w_mlp2[l].astype(BF16),
                  s=s, tc=tc, nb=nb, n_rows=n)
        del with_ctx

    out = _final_norm(xa, final_norm_g[None, :], s=s, tc=tc, nb=nb)
    return out.reshape(nb, t, d)
```

```python
import functools
import math

import jax
import jax.numpy as jnp
from jax import lax
from jax.experimental import pallas as pl
from jax.experimental.pallas import tpu as pltpu

F32 = jnp.float32
BF16 = jnp.bfloat16

GRID_W = 64
MLA_HEADS = 6
MLA_Q_LORA = 768
MLA_KV_LORA = 512
MLA_NOPE_DIM = 128
MLA_ROPE_DIM = 64
MLA_V_DIM = 128
MLA_SCALE = (MLA_NOPE_DIM + MLA_ROPE_DIM) ** -0.5
GDN_HEADS = 6
GDN_K_DIM = 128
GDN_V_DIM = 128
GDN_CONV = 5
GDN_CHUNK = 64
DIFF_HEADS = 4
DIFF_QK_DIM = 64
DIFF_V_DIM = 128
DIFF_SCALE = DIFF_QK_DIM ** -0.5
ROPE_BASE = 10000.0
NORM_EPS = 1e-6
LOG2E = math.log2(math.e)

LANES = 128
MLA_QK_PAD = 2 * LANES
VMEM_LIMIT = 56 * 1024 * 1024
NORM_ROWS = 16
NORM_UNROLL = 4
MLP_CHUNK = 512
GDN_TILE = 4 * GDN_CHUNK

GQKV_W = 2 * GDN_HEADS * GDN_K_DIM + GDN_HEADS * GDN_V_DIM
P_CQ = 0
P_GQKV = P_CQ + MLA_Q_LORA
P_GZ = P_GQKV + GQKV_W
P_KR = P_GZ + GDN_HEADS * GDN_V_DIM
P_GAB = P_KR + LANES
P_CKV = P_GAB + LANES
P_DQ = P_CKV + MLA_KV_LORA
P_DK = P_DQ + DIFF_HEADS * 2 * DIFF_QK_DIM
P_DV = P_DK + DIFF_HEADS * 2 * DIFF_QK_DIM
P_TOTAL = P_DV + DIFF_HEADS * DIFF_V_DIM


def _cparams(sem):
    return pltpu.CompilerParams(dimension_semantics=sem, vmem_limit_bytes=VMEM_LIMIT)


def _row_tile(s, cap):
    for step in (128, 64):
        best = 0
        for t in range(step, min(s, cap) + 1, step):
            if s % t == 0:
                best = t
        if best:
            return best
    raise ValueError(f"no row tile for {s}")


def _col_tile(n, cap):
    best = 0
    for t in range(LANES, min(n, cap) + 1, LANES):
        if n % t == 0:
            best = t
    return best


def _rms(x, g):
    ms = jnp.mean(x * x, axis=-1, keepdims=True)
    return x * lax.rsqrt(ms + NORM_EPS) * g


def _silu(x):
    return x * jax.nn.sigmoid(x)


def _norm_mod_rows(x_ref, g_ref, shl, scl, shc, scc, hn_ref, first_row, tc):
    tm = x_ref.shape[0]
    rc = NORM_ROWS
    g = g_ref[...]
    gs_l = g * (1.0 + scl[...])
    gs_c = g * (1.0 + scc[...])
    sh_l = shl[...]
    sh_c = shc[...]

    def body(r, carry):
        r0 = pl.multiple_of(r * rc, rc)
        is_ctx = first_row + r0 < tc
        x = x_ref[pl.ds(r0, rc), :]
        inv = lax.rsqrt(jnp.mean(x * x, axis=-1, keepdims=True) + NORM_EPS)
        gs = jnp.where(is_ctx, gs_c, gs_l)
        sh = jnp.where(is_ctx, sh_c, sh_l)
        hn_ref[pl.ds(r0, rc), :] = (x * inv * gs + sh).astype(hn_ref.dtype)
        return carry

    lax.fori_loop(0, tm // rc, body, 0, unroll=NORM_UNROLL)


def _gated_residual_rows(x_ref, o_ref, gl_ref, gc_ref, first_row, tc):
    tm = x_ref.shape[0]
    rc = NORM_ROWS
    g_l = gl_ref[...]
    g_c = gc_ref[...]

    def body(r, carry):
        r0 = pl.multiple_of(r * rc, rc)
        gate = jnp.where(first_row + r0 < tc, g_c, g_l)
        rows = pl.ds(r0, rc)
        o_ref[rows, :] = x_ref[rows, :] + gate * o_ref[rows, :]
        return carry

    lax.fori_loop(0, tm // rc, body, 0, unroll=NORM_UNROLL)


def _ctx_rows(i, tpb, tm, tc):
    rows = (i % tpb) * tm + lax.broadcasted_iota(jnp.int32, (tm, 1), 0)
    return rows < tc


def _rot_half(x, quarter):
    ax = x.ndim - 1
    n = x.shape[ax]
    lane = lax.broadcasted_iota(jnp.int32, x.shape, ax)
    nxt = pltpu.roll(x, n - quarter, ax)
    prv = pltpu.roll(x, quarter, ax)
    return jnp.where((lane % (2 * quarter)) < quarter, -nxt, prv)


def _adaln_kernel(s_ref, w_ref, b_ref, o_ref):
    s = _silu(s_ref[...])
    acc = jnp.dot(s.astype(BF16), w_ref[...].astype(BF16), preferred_element_type=F32)
    o_ref[...] = acc + b_ref[...]


def _adaln(s_rows, w_ada, b_ada):
    depth, d, n6 = w_ada.shape
    r = s_rows.shape[0]
    tn = _col_tile(n6, 1024)
    return pl.pallas_call(
        _adaln_kernel,
        grid=(depth, n6 // tn),
        in_specs=[
            pl.BlockSpec((r, d), lambda l, j: (0, 0)),
            pl.BlockSpec((None, d, tn), lambda l, j: (l, 0, j)),
            pl.BlockSpec((None, 1, tn), lambda l, j: (l, 0, j)),
        ],
        out_specs=pl.BlockSpec((None, r, tn), lambda l, j: (l, 0, j)),
        out_shape=jax.ShapeDtypeStruct((depth, r, n6), F32),
        compiler_params=_cparams(("arbitrary", "arbitrary")),
        name="adaln",
    )(s_rows, w_ada, b_ada.reshape(depth, 1, n6))


def _mod_specs(d, tpb, nb, ks):
    specs = []
    for k in ks:
        specs.append(pl.BlockSpec((None, None, 1, d), lambda i, j, k=k: (i // tpb, k, 0, 0)))
    for k in ks:
        specs.append(pl.BlockSpec((None, None, 1, d), lambda i, j, k=k: (nb, k, 0, 0)))
    return specs


def _in_proj_kernel(x_ref, g_ref, shl, scl, shc, scc, w_ref, o_ref, hn_ref, *, tm, tpb, tc):
    i = pl.program_id(0)

    @pl.when(pl.program_id(1) == 0)
    def _():
        _norm_mod_rows(x_ref, g_ref, shl, scl, shc, scc, hn_ref, (i % tpb) * tm, tc)

    o_ref[...] = jnp.dot(hn_ref[...], w_ref[...], preferred_element_type=F32)


def _in_proj(xa, gain, mod, w, *, s, tc, nb):
    n, d = xa.shape
    nout = w.shape[1]
    tm = _row_tile(s, 1152)
    tn = _col_tile(nout, 512)
    tpb = s // tm
    return pl.pallas_call(
        functools.partial(_in_proj_kernel, tm=tm, tpb=tpb, tc=tc),
        grid=(n // tm, nout // tn),
        in_specs=[
            pl.BlockSpec((tm, d), lambda i, j: (i, 0)),
            pl.BlockSpec((1, d), lambda i, j: (0, 0)),
            *_mod_specs(d, tpb, nb, (0, 1)),
            pl.BlockSpec((d, tn), lambda i, j: (0, j)),
        ],
        out_specs=pl.BlockSpec((tm, tn), lambda i, j: (i, j)),
        out_shape=jax.ShapeDtypeStruct((n, nout), F32),
        scratch_shapes=[pltpu.VMEM((tm, d), BF16)],
        compiler_params=_cparams(("arbitrary", "arbitrary")),
        name="in_proj",
    )(xa, gain, mod, mod, mod, mod, w)


def _mla_prep_kernel(cq_ref, ckv_ref, kr_ref, cos_ref, sin_ref, gq_ref, gkv_ref, wq_ref, wk_ref, wvt_ref,
                     q_ref, k_ref, vt_ref):
    cos = cos_ref[...]
    sin = sin_ref[...]
    quarter = MLA_ROPE_DIM // 4

    def rope(x):
        return x * cos + _rot_half(x, quarter) * sin

    qn = _rms(cq_ref[...], gq_ref[...]).astype(BF16)
    q = jnp.dot(qn, wq_ref[...], preferred_element_type=F32)
    kvn = _rms(ckv_ref[...], gkv_ref[...]).astype(BF16)
    kn = jnp.dot(kvn, wk_ref[...], preferred_element_type=F32)
    vt = lax.dot_general(wvt_ref[...], kvn, (((1,), (1,)), ((), ())), preferred_element_type=F32)
    vt_ref[...] = vt.astype(BF16)
    kr = rope(kr_ref[...]).astype(BF16)
    for h in range(MLA_HEADS):
        a = h * MLA_QK_PAD
        q_ref[:, a:a + LANES] = (q[:, a:a + LANES] * (MLA_SCALE * LOG2E)).astype(BF16)
        q_ref[:, a + LANES:a + 2 * LANES] = (rope(q[:, a + LANES:a + 2 * LANES])
                                             * (MLA_SCALE * LOG2E)).astype(BF16)
        k_ref[:, a:a + LANES] = kn[:, h * LANES:(h + 1) * LANES].astype(BF16)
        k_ref[:, a + LANES:a + 2 * LANES] = kr


def _mla_prep(proj, cos_t, sin_t, g_q, g_kv, wq, wk, wvt, *, s):
    n = proj.shape[0]
    tm = _row_tile(s, 768)
    tpb = s // tm
    hq = MLA_HEADS * MLA_QK_PAD
    hv = MLA_HEADS * MLA_V_DIM
    full = lambda a: pl.BlockSpec(a.shape, lambda i: (0,) * a.ndim)
    return pl.pallas_call(
        _mla_prep_kernel,
        grid=(n // tm,),
        in_specs=[
            pl.BlockSpec((tm, MLA_Q_LORA), lambda i: (i, P_CQ // MLA_Q_LORA)),
            pl.BlockSpec((tm, MLA_KV_LORA), lambda i: (i, P_CKV // MLA_KV_LORA)),
            pl.BlockSpec((tm, LANES), lambda i: (i, P_KR // LANES)),
            pl.BlockSpec((tm, LANES), lambda i: (i % tpb, 0)),
            pl.BlockSpec((tm, LANES), lambda i: (i % tpb, 0)),
            full(g_q), full(g_kv), full(wq), full(wk), full(wvt),
        ],
        out_specs=[
            pl.BlockSpec((tm, hq), lambda i: (i, 0)),
            pl.BlockSpec((tm, hq), lambda i: (i, 0)),
            pl.BlockSpec((hv, tm), lambda i: (0, i)),
        ],
        out_shape=[
            jax.ShapeDtypeStruct((n, hq), BF16),
            jax.ShapeDtypeStruct((n, hq), BF16),
            jax.ShapeDtypeStruct((hv, n), BF16),
        ],
        compiler_params=_cparams(("arbitrary",)),
        name="mla_prep",
    )(proj, proj, proj, cos_t, sin_t, g_q, g_kv, wq, wk, wvt)


def _softplus(x):
    return jnp.maximum(x, 0.0) + jnp.log1p(jnp.exp(-jnp.abs(x)))


def _aux_prep_kernel(dq_ref, dk_ref, dv_ref, ab_ref, cos_ref, sin_ref, alog_ref, dtb_ref,
                     q_ref, k_ref, vt_ref, gates_ref, gt_ref, *, tm):
    cos = cos_ref[...]
    sin = sin_ref[...]
    quarter = DIFF_QK_DIM // 4
    for h in range(DIFF_HEADS):
        sl = slice(h * LANES, (h + 1) * LANES)
        xq = dq_ref[:, sl]
        xk = dk_ref[:, sl]
        q_ref[:, sl] = ((xq * cos + _rot_half(xq, quarter) * sin) * (DIFF_SCALE * LOG2E)).astype(BF16)
        k_ref[:, sl] = (xk * cos + _rot_half(xk, quarter) * sin).astype(BF16)
    vt_ref[...] = dv_ref[...].T.astype(BF16)

    ab = ab_ref[...]
    g = -jnp.exp(alog_ref[...]) * _softplus(ab + dtb_ref[...])
    beta = jax.nn.sigmoid(ab)
    row = lax.broadcasted_iota(jnp.int32, (tm, LANES), 0) % GDN_CHUNK
    pre = g
    suf = g
    step = 1
    while step < GDN_CHUNK:
        pre = pre + jnp.where(row >= step, pltpu.roll(pre, step, 0), 0.0)
        suf = suf + jnp.where(row < GDN_CHUNK - step, pltpu.roll(suf, tm - step, 0), 0.0)
        step *= 2
    lane = lax.broadcasted_iota(jnp.int32, (tm, LANES), 1)
    total = pltpu.roll(pre + suf - g, 4 * GDN_HEADS, 1)
    gates = jnp.where(lane < GDN_HEADS, pre,
                      jnp.where(lane < 2 * GDN_HEADS, suf, jnp.where(lane < 4 * GDN_HEADS, beta, total)))
    gates_ref[...] = gates
    gt_ref[...] = gates.T[0:gt_ref.shape[0], :]


def _aux_prep(proj, cos_t, sin_t, alog_row, dtb_row, *, s):
    n = proj.shape[0]
    tm = _row_tile(s, 768)
    tpb = s // tm
    wq = DIFF_HEADS * 2 * DIFF_QK_DIM
    wv = DIFF_HEADS * DIFF_V_DIM
    gt_rows = 32
    return pl.pallas_call(
        functools.partial(_aux_prep_kernel, tm=tm),
        grid=(n // tm,),
        in_specs=[
            pl.BlockSpec((tm, wq), lambda i: (i, P_DQ // wq)),
            pl.BlockSpec((tm, wq), lambda i: (i, P_DK // wq)),
            pl.BlockSpec((tm, wv), lambda i: (i, P_DV // wv)),
            pl.BlockSpec((tm, LANES), lambda i: (i, P_GAB // LANES)),
            pl.BlockSpec((tm, LANES), lambda i: (i % tpb, 0)),
            pl.BlockSpec((tm, LANES), lambda i: (i % tpb, 0)),
            pl.BlockSpec((1, LANES), lambda i: (0, 0)),
            pl.BlockSpec((1, LANES), lambda i: (0, 0)),
        ],
        out_specs=[
            pl.BlockSpec((tm, wq), lambda i: (i, 0)),
            pl.BlockSpec((tm, wq), lambda i: (i, 0)),
            pl.BlockSpec((wv, tm), lambda i: (0, i)),
            pl.BlockSpec((tm, LANES), lambda i: (i, 0)),
            pl.BlockSpec((gt_rows, tm), lambda i: (0, i)),
        ],
        out_shape=[
            jax.ShapeDtypeStruct((n, wq), BF16),
            jax.ShapeDtypeStruct((n, wq), BF16),
            jax.ShapeDtypeStruct((wv, n), BF16),
            jax.ShapeDtypeStruct((n, LANES), F32),
            jax.ShapeDtypeStruct((gt_rows, n), F32),
        ],
        compiler_params=_cparams(("arbitrary",)),
        name="aux_prep",
    )(proj, proj, proj, proj, cos_t, sin_t, alog_row, dtb_row)


def _gdn_prep_kernel(x_ref, w_ref, o_ref, *, s, tc, groups):
    c = pl.program_id(1)
    x = x_ref[...]
    w = w_ref[...]
    t = lax.broadcasted_iota(jnp.int32, (s, 1), 0)
    lo = jnp.where(t < tc, 0, tc)
    hi = jnp.where(t < tc, tc, s)
    half = GDN_CONV // 2
    acc = x * w[half:half + 1, :]
    for off in range(-half, half + 1):
        if off == 0:
            continue
        xs = pltpu.roll(x, (-off) % s, 0)
        ok = (t + off >= lo) & (t + off < hi)
        acc = acc + jnp.where(ok, xs, 0.0) * w[off + half:off + half + 1, :]
    y = _silu(acc)
    qscale = jnp.where(c < groups, GDN_K_DIM ** -0.5, 1.0)
    is_qk = c < 2 * groups
    for gi in range(x.shape[1] // LANES):
        sl = slice(gi * LANES, (gi + 1) * LANES)
        yg = y[:, sl]
        nrm = yg * lax.rsqrt(jnp.sum(yg * yg, axis=-1, keepdims=True) + NORM_EPS) * qscale
        o_ref[:, sl] = jnp.where(is_qk, nrm, yg)


def _gdn_prep(proj, conv_w, *, s, tc, nb):
    n = proj.shape[0]
    cw = 3 * LANES
    groups = GDN_HEADS * GDN_K_DIM // cw
    off = P_GQKV // cw
    return pl.pallas_call(
        functools.partial(_gdn_prep_kernel, s=s, tc=tc, groups=groups),
        grid=(nb, GQKV_W // cw),
        in_specs=[
            pl.BlockSpec((s, cw), lambda b, c: (b, off + c)),
            pl.BlockSpec((GDN_CONV, cw), lambda b, c: (0, c)),
        ],
        out_specs=pl.BlockSpec((s, cw), lambda b, c: (b, c)),
        out_shape=jax.ShapeDtypeStruct((n, GQKV_W), F32),
        compiler_params=_cparams(("arbitrary", "arbitrary")),
        name="gdn_prep",
    )(proj, conv_w)


def _gdn_solve_kernel(q_ref, k_ref, v_ref, gates_ref, gt_ref, u_ref, w_ref, qg_ref, kd_ref, in_ref,
                      l_scr, x_scr, rhs_scr):
    t = q_ref.shape[0]
    c = GDN_CHUNK
    nh = GDN_HEADS
    ii = lax.broadcasted_iota(jnp.int32, (t, t), 0)
    jj = lax.broadcasted_iota(jnp.int32, (t, t), 1)
    log2c = c.bit_length() - 1
    same = (ii >> log2c) == (jj >> log2c)
    eye = jnp.where(ii == jj, 1.0, 0.0)
    level_masks = [((ii >> (sh + 1)) == (jj >> (sh + 1))) & ((ii >> sh) != (jj >> sh))
                   for sh in range(1, log2c)]
    pair = (ii >> 1) == (jj >> 1)
    gates = gates_ref[...]
    gt = gt_ref[...]
    nt = (((1,), (1,)), ((), ()))
    for h in range(nh):
        sl = slice(h * LANES, (h + 1) * LANES)
        q = q_ref[:, sl]
        k = k_ref[:, sl]
        v = v_ref[:, sl]
        kb = k.astype(BF16)
        kk = lax.dot_general(kb, kb, nt, preferred_element_type=F32)
        qk = lax.dot_general(q.astype(BF16), kb, nt, preferred_element_type=F32)
        for d in range(2):
            fwd = d == 0
            gi = d * nh + h
            gcol = gates[:, gi:gi + 1]
            bcol = gates[:, 2 * nh + gi:2 * nh + gi + 1]
            glcol = gates[:, 4 * nh + gi:4 * nh + gi + 1]
            grow = gt[gi:gi + 1, :]
            incl = same & ((ii >= jj) if fwd else (ii <= jj))
            strict = same & ((ii > jj) if fwd else (ii < jj))
            decay = jnp.where(incl, jnp.exp(jnp.where(incl, gcol - grow, 0.0)), 0.0)
            lmat = jnp.where(strict, bcol * kk * decay, 0.0)
            l_scr[gi] = lmat.astype(BF16)
            x_scr[gi] = (eye - jnp.where(pair, lmat, 0.0)).astype(BF16)
            eg = jnp.exp(gcol)
            rhs_scr[gi] = jnp.concatenate([v * bcol, k * (bcol * eg)], axis=1).astype(BF16)
            cs = slice(gi * LANES, (gi + 1) * LANES)
            qg_ref[:, cs] = (q * eg).astype(BF16)
            kd_ref[:, cs] = (k * jnp.exp(glcol - gcol)).astype(BF16)
            intra = qk * decay
            compact = jnp.concatenate([intra[n * c:(n + 1) * c, n * c:(n + 1) * c] for n in range(t // c)],
                                      axis=0)
            in_ref[:, cs] = jnp.concatenate([compact, jnp.zeros((t, LANES - c), F32)], axis=1).astype(BF16)
    zero = jnp.zeros((t, t), BF16)
    for m in level_masks:
        for gi in range(2 * nh):
            xb = x_scr[gi]
            nx = jnp.dot(jnp.where(m, l_scr[gi], zero), xb, preferred_element_type=F32)
            z = jnp.dot(xb, nx.astype(BF16), preferred_element_type=F32)
            x_scr[gi] = xb - z.astype(BF16)
    for gi in range(2 * nh):
        uw = jnp.dot(x_scr[gi], rhs_scr[gi], preferred_element_type=F32)
        cs = slice(gi * LANES, (gi + 1) * LANES)
        u_ref[:, cs] = uw[:, :GDN_V_DIM]
        w_ref[:, cs] = uw[:, GDN_V_DIM:].astype(BF16)


def _gdn_solve(qkv, gates, gt, *, s, nb):
    n = qkv.shape[0]
    t = GDN_TILE
    nt = s // t
    hw = GDN_HEADS * GDN_K_DIM
    cw = 2 * hw
    row = lambda b, i: (b * nt + i, 0)
    return pl.pallas_call(
        _gdn_solve_kernel,
        grid=(nb, nt),
        in_specs=[
            pl.BlockSpec((t, hw), lambda b, i: (b * nt + i, 0)),
            pl.BlockSpec((t, hw), lambda b, i: (b * nt + i, 1)),
            pl.BlockSpec((t, hw), lambda b, i: (b * nt + i, 2)),
            pl.BlockSpec((t, LANES), row),
            pl.BlockSpec((gt.shape[0], t), lambda b, i: (0, b * nt + i)),
        ],
        out_specs=[pl.BlockSpec((t, cw), row)] * 5,
        out_shape=[jax.ShapeDtypeStruct((n, cw), F32)] + [jax.ShapeDtypeStruct((n, cw), BF16)] * 4,
        scratch_shapes=[
            pltpu.VMEM((2 * GDN_HEADS, t, t), BF16),
            pltpu.VMEM((2 * GDN_HEADS, t, t), BF16),
            pltpu.VMEM((2 * GDN_HEADS, t, GDN_V_DIM + GDN_K_DIM), BF16),
        ],
        compiler_params=_cparams(("arbitrary", "arbitrary")),
        name="gdn_solve",
    )(qkv, qkv, qkv, gates, gt)


def _gdn_scan_kernel(uf, ub, wf, wb, qgf, qgb, kdf, kdb, inf, inb, gf_ref, gb_ref, of_ref, ob_ref, s_ref):
    @pl.when(pl.program_id(1) == 0)
    def _():
        s_ref[...] = jnp.zeros_like(s_ref)

    c = GDN_CHUNK
    nh = GDN_HEADS
    npt = uf.shape[0] // c
    tn = (((0,), (0,)), ((), ()))
    dirs = ((uf, wf, qgf, kdf, inf, gf_ref, of_ref), (ub, wb, qgb, kdb, inb, gb_ref, ob_ref))
    for p in range(npt):
        chains = []
        for d, (u_ref, w_ref, qg_ref, kd_ref, in_ref, g_ref, o_ref) in enumerate(dirs):
            pc = p if d == 0 else npt - 1 - p
            rs = slice(pc * c, (pc + 1) * c)
            totals = jnp.exp(g_ref[pc * c:pc * c + 1, :])
            for h in range(nh):
                j = d * nh + h
                cs = slice(h * LANES, (h + 1) * LANES)
                chains.append((j, rs, cs, u_ref, w_ref, qg_ref, kd_ref, in_ref, o_ref,
                               totals[:, 4 * nh + j:4 * nh + j + 1]))
        states = [s_ref[ch[0]] for ch in chains]
        prods = []
        for st, (j, rs, cs, u_ref, w_ref, qg_ref, kd_ref, in_ref, o_ref, decay) in zip(states, chains):
            wq = jnp.concatenate([w_ref[rs, cs], qg_ref[rs, cs]], axis=0)
            prods.append(jnp.dot(wq, st.astype(BF16), preferred_element_type=F32))
        v_news = [(ch[3][ch[1], ch[2]] - r[:c]).astype(BF16) for r, ch in zip(prods, chains)]
        for st, r, v_new, (j, rs, cs, u_ref, w_ref, qg_ref, kd_ref, in_ref, o_ref, decay) in zip(
                states, prods, v_news, chains):
            intra = in_ref[rs, cs][:, :c]
            o_ref[rs, cs] = r[c:] + jnp.dot(intra, v_new, preferred_element_type=F32)
            s_ref[j] = st * decay + lax.dot_general(kd_ref[rs, cs], v_new, tn, preferred_element_type=F32)


def _gdn_scan(u, w, qg, kd, intra, gates, *, s, tc, nb):
    n = u.shape[0]
    t = GDN_TILE
    nt = s // t
    nct = tc // t
    hw = GDN_HEADS * GDN_V_DIM

    def bwd_tile(i):
        return jnp.where(i < nct, nct - 1 - i, nct + nt - 1 - i)

    fwd = lambda b, i: (b * nt + i, 0)
    bwd = lambda b, i: (b * nt + bwd_tile(i), 1)
    pair = [pl.BlockSpec((t, hw), fwd), pl.BlockSpec((t, hw), bwd)]
    return pl.pallas_call(
        _gdn_scan_kernel,
        grid=(nb, nt),
        in_specs=pair * 5 + [
            pl.BlockSpec((t, LANES), fwd),
            pl.BlockSpec((t, LANES), lambda b, i: (b * nt + bwd_tile(i), 0)),
        ],
        out_specs=[
            pl.BlockSpec((t, hw), fwd),
            pl.BlockSpec((t, hw), lambda b, i: (b * nt + bwd_tile(i), 0)),
        ],
        out_shape=[jax.ShapeDtypeStruct((n, hw), F32)] * 2,
        scratch_shapes=[pltpu.VMEM((2 * GDN_HEADS, GDN_K_DIM, GDN_V_DIM), F32)],
        compiler_params=_cparams(("arbitrary", "arbitrary")),
        name="gdn_scan",
    )(u, u, w, w, qg, qg, kd, kd, intra, intra, gates, gates)


def _gdn_out_kernel(of_ref, ob_ref, z_ref, gain_ref, o_ref):
    gain = gain_ref[...]
    for h in range(GDN_HEADS):
        sl = slice(h * LANES, (h + 1) * LANES)
        o = of_ref[:, sl] + ob_ref[:, sl]
        o_ref[:, sl] = (_rms(o, gain) * _silu(z_ref[:, sl])).astype(o_ref.dtype)


def _gdn_out(o_f, o_b, proj, gain, *, s):
    n, hw = o_f.shape
    tm = _row_tile(s, 768)
    return pl.pallas_call(
        _gdn_out_kernel,
        grid=(n // tm,),
        in_specs=[
            pl.BlockSpec((tm, hw), lambda i: (i, 0)),
            pl.BlockSpec((tm, hw), lambda i: (i, 0)),
            pl.BlockSpec((tm, hw), lambda i: (i, P_GZ // hw)),
            pl.BlockSpec((1, LANES), lambda i: (0, 0)),
        ],
        out_specs=pl.BlockSpec((tm, hw), lambda i: (i, 0)),
        out_shape=jax.ShapeDtypeStruct((n, hw), BF16),
        compiler_params=_cparams(("arbitrary",)),
        name="gdn_out",
    )(o_f, o_b, proj, gain)


def _softmax2_t(st):
    m = jnp.max(st, axis=0, keepdims=True)
    e = jnp.exp2(st - m)
    return e, 1.0 / jnp.sum(e, axis=0, keepdims=True)


def _pipelined_query_tiles(nq, scores, finish):
    scores(0)
    for t in range(nq):
        if t + 1 < nq:
            scores(t + 1)
        finish(t)


def _mla_attn_kernel(q_ref, k_ref, vt_ref, o_ref, s_scr, *, tc, tq):
    nt = (((1,), (1,)), ((), ()))
    s = q_ref.shape[0]
    keys = lambda t: tc if t * tq < tc else s

    def scores(t):
        kn = keys(t)
        s_scr[t % 2, 0:kn, :] = lax.dot_general(k_ref[0:kn, :], q_ref[t * tq:(t + 1) * tq, :], nt,
                                                preferred_element_type=F32)

    def finish(t):
        kn = keys(t)
        e, rinv = _softmax2_t(s_scr[t % 2, 0:kn, :])
        ot = jnp.dot(vt_ref[:, 0:kn], e.astype(BF16), preferred_element_type=F32) * rinv
        o_ref[t * tq:(t + 1) * tq, :] = ot.T.astype(o_ref.dtype)

    _pipelined_query_tiles(s // tq, scores, finish)


def _mla_attn(q, k, vt, *, s, tc, nb):
    n = q.shape[0]
    tq = _row_tile(tc, 256)
    return pl.pallas_call(
        functools.partial(_mla_attn_kernel, tc=tc, tq=tq),
        grid=(nb, MLA_HEADS),
        in_specs=[
            pl.BlockSpec((s, MLA_QK_PAD), lambda b, h: (b, h)),
            pl.BlockSpec((s, MLA_QK_PAD), lambda b, h: (b, h)),
            pl.BlockSpec((MLA_V_DIM, s), lambda b, h: (h, b)),
        ],
        out_specs=pl.BlockSpec((s, MLA_V_DIM), lambda b, h: (b, h)),
        out_shape=jax.ShapeDtypeStruct((n, MLA_HEADS * MLA_V_DIM), BF16),
        scratch_shapes=[pltpu.VMEM((2, s, tq), F32)],
        compiler_params=_cparams(("arbitrary", "arbitrary")),
        name="mla_attn",
    )(q, k, vt)


def _diff_attn_kernel(q_ref, k_ref, vt_ref, lam_ref, gain_ref, o_ref, s_scr, *, tc, tq, lam_init):
    nt = (((1,), (1,)), ((), ()))
    s = q_ref.shape[0]
    keys = lambda t: tc if t * tq < tc else s
    lp = lam_ref[...]
    lam = (jnp.exp(jnp.sum(lp[0:1] * lp[1:2], axis=-1, keepdims=True))
           - jnp.exp(jnp.sum(lp[2:3] * lp[3:4], axis=-1, keepdims=True)) + lam_init)
    gain = gain_ref[...] * (1.0 - lam_init)
    lane = lax.broadcasted_iota(jnp.int32, (tq, LANES), 1)

    def scores(t):
        kn = keys(t)
        q = q_ref[t * tq:(t + 1) * tq, :]
        zero = jnp.zeros_like(q)
        for g in range(2):
            qg = jnp.where((lane < DIFF_QK_DIM) == (g == 0), q, zero)
            s_scr[2 * (t % 2) + g, 0:kn, :] = lax.dot_general(k_ref[0:kn, :], qg, nt,
                                                              preferred_element_type=F32)

    def finish(t):
        kn = keys(t)
        e1, r1 = _softmax2_t(s_scr[2 * (t % 2), 0:kn, :])
        e2, r2 = _softmax2_t(s_scr[2 * (t % 2) + 1, 0:kn, :])
        p = e1 * r1 - e2 * (r2 * lam)
        ot = jnp.dot(vt_ref[:, 0:kn], p.astype(BF16), preferred_element_type=F32)
        ms = jnp.mean(ot * ot, axis=0, keepdims=True)
        ot = ot * lax.rsqrt(ms + NORM_EPS) * gain
        o_ref[t * tq:(t + 1) * tq, :] = ot.T.astype(o_ref.dtype)

    _pipelined_query_tiles(s // tq, scores, finish)


def _diff_attn(q, k, vt, lam_p, gain_col, *, s, tc, nb, lam_init):
    n = q.shape[0]
    tq = _row_tile(tc, 256)
    return pl.pallas_call(
        functools.partial(_diff_attn_kernel, tc=tc, tq=tq, lam_init=lam_init),
        grid=(nb, DIFF_HEADS),
        in_specs=[
            pl.BlockSpec((s, LANES), lambda b, h: (b, h)),
            pl.BlockSpec((s, LANES), lambda b, h: (b, h)),
            pl.BlockSpec((DIFF_V_DIM, s), lambda b, h: (h, b)),
            pl.BlockSpec(lam_p.shape, lambda b, h: (0, 0)),
            pl.BlockSpec(gain_col.shape, lambda b, h: (0, 0)),
        ],
        out_specs=pl.BlockSpec((s, DIFF_V_DIM), lambda b, h: (b, h)),
        out_shape=jax.ShapeDtypeStruct((n, DIFF_HEADS * DIFF_V_DIM), BF16),
        scratch_shapes=[pltpu.VMEM((4, s, tq), F32)],
        compiler_params=_cparams(("arbitrary", "arbitrary")),
        name="diff_attn",
    )(q, k, vt, lam_p, gain_col)


def _out_proj_kernel(x_ref, a1_ref, a2_ref, a3_ref, w1_ref, w2_ref, w3_ref, gl_ref, gc_ref, o_ref,
                     *, tm, tpb, tc):
    i = pl.program_id(0)
    acc = jnp.dot(a1_ref[...], w1_ref[...], preferred_element_type=F32)
    acc += jnp.dot(a2_ref[...], w2_ref[...], preferred_element_type=F32)
    acc += jnp.dot(a3_ref[...], w3_ref[...], preferred_element_type=F32)
    gate = jnp.where(_ctx_rows(i, tpb, tm, tc), gc_ref[...], gl_ref[...])
    o_ref[...] = x_ref[...] + gate * acc


def _out_proj(xa, a1, a2, a3, w, mod, *, s, tc, nb):
    n, d = xa.shape
    tm = _row_tile(s, 1152)
    tn = _col_tile(d, 512)
    tpb = s // tm
    k1, k2, k3 = a1.shape[1], a2.shape[1], a3.shape[1]
    assert k1 == k2 and (k1 + k2) % k3 == 0
    return pl.pallas_call(
        functools.partial(_out_proj_kernel, tm=tm, tpb=tpb, tc=tc),
        grid=(n // tm, d // tn),
        in_specs=[
            pl.BlockSpec((tm, tn), lambda i, j: (i, j)),
            pl.BlockSpec((tm, k1), lambda i, j: (i, 0)),
            pl.BlockSpec((tm, k2), lambda i, j: (i, 0)),
            pl.BlockSpec((tm, k3), lambda i, j: (i, 0)),
            pl.BlockSpec((k1, tn), lambda i, j: (0, j)),
            pl.BlockSpec((k2, tn), lambda i, j: (1, j)),
            pl.BlockSpec((k3, tn), lambda i, j: ((k1 + k2) // k3, j)),
            pl.BlockSpec((None, None, 1, tn), lambda i, j: (i // tpb, 2, 0, j)),
            pl.BlockSpec((None, None, 1, tn), lambda i, j: (nb, 2, 0, j)),
        ],
        out_specs=pl.BlockSpec((tm, tn), lambda i, j: (i, j)),
        out_shape=jax.ShapeDtypeStruct((n, d), F32),
        compiler_params=_cparams(("arbitrary", "arbitrary")),
        name="out_proj",
    )(xa, a1, a2, a3, w, w, w, mod, mod)


def _mlp_kernel(x_ref, g_ref, shl, scl, shc, scc, gl_ref, gc_ref, w1_ref, w2_ref, o_ref, hn_ref,
                *, tm, tpb, tc):
    i = pl.program_id(0)
    j = pl.program_id(1)
    first_row = (i % tpb) * tm

    @pl.when(j == 0)
    def _():
        _norm_mod_rows(x_ref, g_ref, shl, scl, shc, scc, hn_ref, first_row, tc)
        o_ref[...] = jnp.zeros_like(o_ref)

    tf = w1_ref.shape[1]
    d = o_ref.shape[1]
    hn = hn_ref[...]
    cf = math.gcd(tf, MLP_CHUNK)
    cd = math.gcd(d, MLP_CHUNK)
    acts = []
    for c0 in range(0, tf, cf):
        a = jnp.dot(hn, w1_ref[:, c0:c0 + cf], preferred_element_type=F32)
        acts.append(jnp.square(jnp.maximum(a, 0.0)).astype(BF16))
    act = jnp.concatenate(acts, axis=1) if len(acts) > 1 else acts[0]
    for n0 in range(0, d, cd):
        o_ref[:, n0:n0 + cd] += jnp.dot(act, w2_ref[:, n0:n0 + cd], preferred_element_type=F32)

    @pl.when(j == pl.num_programs(1) - 1)
    def _():
        _gated_residual_rows(x_ref, o_ref, gl_ref, gc_ref, first_row, tc)


def _mlp(xa, gain, mod, w1, w2, *, s, tc, nb, n_rows):
    d = xa.shape[1]
    dff = w1.shape[1]
    tm = _row_tile(s, 768)
    tf = _col_tile(dff, 1024)
    tpb = s // tm
    mod_gate = lambda row: pl.BlockSpec((None, None, 1, d), row)
    return pl.pallas_call(
        functools.partial(_mlp_kernel, tm=tm, tpb=tpb, tc=tc),
        grid=(n_rows // tm, dff // tf),
        in_specs=[
            pl.BlockSpec((tm, d), lambda i, j: (i, 0)),
            pl.BlockSpec((1, d), lambda i, j: (0, 0)),
            *_mod_specs(d, tpb, nb, (3, 4)),
            mod_gate(lambda i, j: (i // tpb, 5, 0, 0)),
            mod_gate(lambda i, j: (nb, 5, 0, 0)),
            pl.BlockSpec((d, tf), lambda i, j: (0, j)),
            pl.BlockSpec((tf, d), lambda i, j: (j, 0)),
        ],
        out_specs=pl.BlockSpec((tm, d), lambda i, j: (i, 0)),
        out_shape=jax.ShapeDtypeStruct((n_rows, d), F32),
        scratch_shapes=[pltpu.VMEM((tm, d), BF16)],
        compiler_params=_cparams(("arbitrary", "arbitrary")),
        name="mlp",
    )(xa, gain, mod, mod, mod, mod, mod, mod, w1, w2)


def _final_norm_kernel(x_ref, g_ref, o_ref):
    o_ref[...] = _rms(x_ref[...], g_ref[...])


def _final_norm(xa, gain, *, s, tc, nb):
    d = xa.shape[1]
    t = s - tc
    tr = _row_tile(math.gcd(tc, t), 512)
    per_b = s // tr
    skip = tc // tr
    nt = t // tr
    return pl.pallas_call(
        _final_norm_kernel,
        grid=(nb, nt),
        in_specs=[
            pl.BlockSpec((tr, d), lambda b, r: (b * per_b + skip + r, 0)),
            pl.BlockSpec((1, d), lambda b, r: (0, 0)),
        ],
        out_specs=pl.BlockSpec((tr, d), lambda b, r: (b * nt + r, 0)),
        out_shape=jax.ShapeDtypeStruct((nb * t, d), F32),
        compiler_params=_cparams(("arbitrary", "arbitrary")),
        name="final_norm",
    )(xa, gain)


def _rope_tables(t, tc, dim, reps):
    rows = t // GRID_W
    row_pos = jnp.repeat(jnp.arange(rows, dtype=F32), GRID_W)
    col_pos = jnp.tile(jnp.arange(GRID_W, dtype=F32), rows)
    quarter = dim // 4
    inv_freq = ROPE_BASE ** (-jnp.arange(quarter, dtype=F32) / quarter)
    ang_r = row_pos[:, None] * inv_freq[None, :]
    ang_c = col_pos[:, None] * inv_freq[None, :]
    ang = jnp.concatenate([ang_r, ang_r, ang_c, ang_c], axis=-1)
    cos = jnp.tile(jnp.cos(ang), (1, reps))
    sin = jnp.tile(jnp.sin(ang), (1, reps))
    pad = LANES - cos.shape[1]
    cos = jnp.pad(cos, ((0, 0), (0, pad)), constant_values=1.0)
    sin = jnp.pad(sin, ((0, 0), (0, pad)))
    cos = jnp.concatenate([jnp.ones((tc, LANES), F32), cos], axis=0)
    sin = jnp.concatenate([jnp.zeros((tc, LANES), F32), sin], axis=0)
    return cos, sin


def _permute_w_in(w):
    d = w.shape[0]
    sizes = (MLA_Q_LORA, MLA_KV_LORA, MLA_ROPE_DIM, GDN_HEADS * GDN_K_DIM, GDN_HEADS * GDN_K_DIM,
             GDN_HEADS * GDN_V_DIM, GDN_HEADS * GDN_V_DIM, 4 * GDN_HEADS,
             DIFF_HEADS * 2 * DIFF_QK_DIM, DIFF_HEADS * 2 * DIFF_QK_DIM, DIFF_HEADS * DIFF_V_DIM)
    offs = [0]
    for sz in sizes:
        offs.append(offs[-1] + sz)
    cq, ckv, kr, gq, gk, gv, gz, gab, dq, dk, dv = (w[:, offs[i]:offs[i + 1]] for i in range(len(sizes)))
    zpad = lambda a: jnp.pad(a, ((0, 0), (0, LANES - a.shape[1])))
    out = jnp.concatenate([cq, gq, gk, gv, gz, zpad(kr), zpad(gab), ckv, dq, dk, dv], axis=1)
    assert out.shape == (d, P_TOTAL)
    return out.astype(BF16)


def kernel(x, c, ctx, c_ctx, w_ada, b_ada, norm1_g, norm2_g, w_in, mla_q_norm_g, mla_kv_norm_g, mla_w_uq, mla_w_ukv, gdn_conv_w, gdn_a_log, gdn_dt_bias, gdn_norm_g, diff_lambda, diff_norm_g, w_out, w_mlp1, w_mlp2, final_norm_g):
    nb, t, d = x.shape
    tc = ctx.shape[1]
    depth = w_ada.shape[0]
    s = tc + t
    n = nb * s
    assert t % GRID_W == 0 and t % GDN_TILE == 0 and tc % GDN_TILE == 0 and tc % NORM_ROWS == 0

    xa = jnp.concatenate([ctx, x], axis=1).reshape(n, d)

    r = -(-(nb + 1) // 8) * 8
    s_rows = jnp.concatenate([c, c_ctx[None, :], jnp.zeros((r - nb - 1, d), F32)], axis=0)
    mod_all = _adaln(s_rows, w_ada, b_ada).reshape(depth, r, 6, 1, d)

    cos_m, sin_m = _rope_tables(t, tc, MLA_ROPE_DIM, 1)
    cos_d, sin_d = _rope_tables(t, tc, DIFF_QK_DIM, 2)

    for l in range(depth):
        mod = mod_all[l]
        lam_init = 0.8 - 0.6 * math.exp(-0.3 * l)

        w_in_p = _permute_w_in(w_in[l])
        wq = mla_w_uq[l].reshape(MLA_Q_LORA, MLA_HEADS, MLA_NOPE_DIM + MLA_ROPE_DIM)
        wq = jnp.pad(wq, ((0, 0), (0, 0), (0, MLA_QK_PAD - wq.shape[2])))
        wq = wq.reshape(MLA_Q_LORA, MLA_HEADS * MLA_QK_PAD).astype(BF16)
        wkv = mla_w_ukv[l].reshape(MLA_KV_LORA, MLA_HEADS, MLA_NOPE_DIM + MLA_V_DIM)
        wk = wkv[:, :, :MLA_NOPE_DIM].reshape(MLA_KV_LORA, MLA_HEADS * MLA_NOPE_DIM).astype(BF16)
        wvt = wkv[:, :, MLA_NOPE_DIM:].reshape(MLA_KV_LORA, MLA_HEADS * MLA_V_DIM).T.astype(BF16)
        gate_pad = LANES - 2 * GDN_HEADS
        alog_row = jnp.pad(gdn_a_log[l].reshape(1, 2 * GDN_HEADS), ((0, 0), (0, gate_pad)))
        dtb_row = jnp.pad(gdn_dt_bias[l].reshape(1, 2 * GDN_HEADS), ((0, 0), (0, gate_pad)))

        proj = _in_proj(xa, norm1_g[l][None, :], mod, w_in_p, s=s, tc=tc, nb=nb)

        mq, mk, mvt = _mla_prep(proj, cos_m, sin_m, mla_q_norm_g[l][None, :], mla_kv_norm_g[l][None, :],
                                wq, wk, wvt, s=s)
        mla_o = _mla_attn(mq, mk, mvt, s=s, tc=tc, nb=nb)

        dq, dk, dvt, gates, gt = _aux_prep(proj, cos_d, sin_d, alog_row, dtb_row, s=s)
        diff_o = _diff_attn(dq, dk, dvt, diff_lambda[l], diff_norm_g[l][:, None], s=s, tc=tc, nb=nb,
                            lam_init=lam_init)

        qkv = _gdn_prep(proj, gdn_conv_w[l], s=s, tc=tc, nb=nb)
        gu, gw, gqg, gkd, gin = _gdn_solve(qkv, gates, gt, s=s, nb=nb)
        o_f, o_b = _gdn_scan(gu, gw, gqg, gkd, gin, gates, s=s, tc=tc, nb=nb)
        gdn_o = _gdn_out(o_f, o_b, proj, gdn_norm_g[l][None, :], s=s)

        xa = _out_proj(xa, mla_o, gdn_o, diff_o, w_out[l].astype(BF16), mod, s=s, tc=tc, nb=nb)
        xa = _mlp(xa, norm2_g[l][None, :], mod, w_mlp1[l].astype(BF16), w_mlp2[l].astype(BF16),
                  s=s, tc=tc, nb=nb, n_rows=n)

    out = _final_norm(xa, final_norm_g[None, :], s=s, tc=tc, nb=nb)
    return out.reshape(nb, t, d)
```

```python
import functools
import math

import jax
import jax.numpy as jnp
from jax import lax
from jax.experimental import pallas as pl
from jax.experimental.pallas import tpu as pltpu

F32 = jnp.float32
BF16 = jnp.bfloat16

GRID_W = 64
MLA_HEADS = 6
MLA_Q_LORA = 768
MLA_KV_LORA = 512
MLA_NOPE_DIM = 128
MLA_ROPE_DIM = 64
MLA_V_DIM = 128
MLA_SCALE = (MLA_NOPE_DIM + MLA_ROPE_DIM) ** -0.5
GDN_HEADS = 6
GDN_K_DIM = 128
GDN_V_DIM = 128
GDN_CONV = 5
GDN_CHUNK = 64
DIFF_HEADS = 4
DIFF_QK_DIM = 64
DIFF_V_DIM = 128
DIFF_SCALE = DIFF_QK_DIM ** -0.5
ROPE_BASE = 10000.0
NORM_EPS = 1e-6
LOG2E = math.log2(math.e)

LANES = 128
MLA_QK_PAD = 2 * LANES
VMEM_LIMIT = 56 * 1024 * 1024
NORM_ROWS = 16
NORM_UNROLL = 4
ATTN_AHEAD = 2
ATTN_BUFS = ATTN_AHEAD + 1
MLP_CHUNK = 512
GDN_TILE = 4 * GDN_CHUNK

GQKV_W = 2 * GDN_HEADS * GDN_K_DIM + GDN_HEADS * GDN_V_DIM
P_CQ = 0
P_GQKV = P_CQ + MLA_Q_LORA
P_GZ = P_GQKV + GQKV_W
P_KR = P_GZ + GDN_HEADS * GDN_V_DIM
P_GAB = P_KR + LANES
P_CKV = P_GAB + LANES
P_DQ = P_CKV + MLA_KV_LORA
P_DK = P_DQ + DIFF_HEADS * 2 * DIFF_QK_DIM
P_DV = P_DK + DIFF_HEADS * 2 * DIFF_QK_DIM
P_TOTAL = P_DV + DIFF_HEADS * DIFF_V_DIM
P_SIDE = 2 * LANES


def _cparams(sem):
    return pltpu.CompilerParams(dimension_semantics=sem, vmem_limit_bytes=VMEM_LIMIT)


def _row_tile(s, cap):
    for step in (128, 64):
        best = 0
        for t in range(step, min(s, cap) + 1, step):
            if s % t == 0:
                best = t
        if best:
            return best
    raise ValueError(f"no row tile for {s}")


def _col_tile(n, cap):
    best = 0
    for t in range(LANES, min(n, cap) + 1, LANES):
        if n % t == 0:
            best = t
    return best


def _rms(x, g):
    ms = jnp.mean(x * x, axis=-1, keepdims=True)
    return x * lax.rsqrt(ms + NORM_EPS) * g


def _silu(x):
    return x * jax.nn.sigmoid(x)


def _norm_mod_rows(x_ref, g_ref, shl, scl, shc, scc, hn_ref, first_row, tc):
    tm = x_ref.shape[0]
    rc = NORM_ROWS
    g = g_ref[...]
    gs_l = g * (1.0 + scl[...])
    gs_c = g * (1.0 + scc[...])
    sh_l = shl[...]
    sh_c = shc[...]

    def body(r, carry):
        r0 = pl.multiple_of(r * rc, rc)
        is_ctx = first_row + r0 < tc
        x = x_ref[pl.ds(r0, rc), :]
        inv = lax.rsqrt(jnp.mean(x * x, axis=-1, keepdims=True) + NORM_EPS)
        gs = jnp.where(is_ctx, gs_c, gs_l)
        sh = jnp.where(is_ctx, sh_c, sh_l)
        hn_ref[pl.ds(r0, rc), :] = (x * inv * gs + sh).astype(hn_ref.dtype)
        return carry

    lax.fori_loop(0, tm // rc, body, 0, unroll=NORM_UNROLL)


def _gated_residual_rows(x_ref, o_ref, gl_ref, gc_ref, first_row, tc):
    tm = x_ref.shape[0]
    rc = NORM_ROWS
    g_l = gl_ref[...]
    g_c = gc_ref[...]

    def body(r, carry):
        r0 = pl.multiple_of(r * rc, rc)
        gate = jnp.where(first_row + r0 < tc, g_c, g_l)
        rows = pl.ds(r0, rc)
        o_ref[rows, :] = x_ref[rows, :] + gate * o_ref[rows, :]
        return carry

    lax.fori_loop(0, tm // rc, body, 0, unroll=NORM_UNROLL)


def _ctx_rows(i, tpb, tm, tc):
    rows = (i % tpb) * tm + lax.broadcasted_iota(jnp.int32, (tm, 1), 0)
    return rows < tc


def _rot_half(x, quarter):
    ax = x.ndim - 1
    n = x.shape[ax]
    lane = lax.broadcasted_iota(jnp.int32, x.shape, ax)
    nxt = pltpu.roll(x, n - quarter, ax)
    prv = pltpu.roll(x, quarter, ax)
    return jnp.where((lane % (2 * quarter)) < quarter, -nxt, prv)


def _adaln_kernel(s_ref, w_ref, b_ref, o_ref):
    s = _silu(s_ref[...])
    acc = jnp.dot(s.astype(BF16), w_ref[...].astype(BF16), preferred_element_type=F32)
    o_ref[...] = acc + b_ref[...]


def _adaln(s_rows, w_ada, b_ada):
    depth, d, n6 = w_ada.shape
    r = s_rows.shape[0]
    tn = _col_tile(n6, 1024)
    return pl.pallas_call(
        _adaln_kernel,
        grid=(depth, n6 // tn),
        in_specs=[
            pl.BlockSpec((r, d), lambda l, j: (0, 0)),
            pl.BlockSpec((None, d, tn), lambda l, j: (l, 0, j)),
            pl.BlockSpec((None, 1, tn), lambda l, j: (l, 0, j)),
        ],
        out_specs=pl.BlockSpec((None, r, tn), lambda l, j: (l, 0, j)),
        out_shape=jax.ShapeDtypeStruct((depth, r, n6), F32),
        compiler_params=_cparams(("arbitrary", "arbitrary")),
        name="adaln",
    )(s_rows, w_ada, b_ada.reshape(depth, 1, n6))


def _mod_specs(d, tpb, nb, ks):
    specs = []
    for k in ks:
        specs.append(pl.BlockSpec((None, None, 1, d), lambda i, j, k=k: (i // tpb, k, 0, 0)))
    for k in ks:
        specs.append(pl.BlockSpec((None, None, 1, d), lambda i, j, k=k: (nb, k, 0, 0)))
    return specs


def _in_proj_kernel(x_ref, g_ref, shl, scl, shc, scc, w_ref, o_ref, side_ref, hn_ref, *, tm, tpb, tc, tn):
    i = pl.program_id(0)
    j = pl.program_id(1)

    @pl.when(j == 0)
    def _():
        _norm_mod_rows(x_ref, g_ref, shl, scl, shc, scc, hn_ref, (i % tpb) * tm, tc)

    acc = jnp.dot(hn_ref[...], w_ref[...], preferred_element_type=F32)
    o_ref[...] = acc.astype(o_ref.dtype)

    @pl.when(j == P_KR // tn)
    def _():
        off = P_KR % tn
        side_ref[...] = acc[:, off:off + P_SIDE]


def _in_proj(xa, gain, mod, w, *, s, tc, nb):
    n, d = xa.shape
    nout = w.shape[1]
    tm = _row_tile(s, 1152)
    tn = _col_tile(nout, 1024)
    assert P_KR // tn == (P_KR + P_SIDE - 1) // tn
    tpb = s // tm
    return pl.pallas_call(
        functools.partial(_in_proj_kernel, tm=tm, tpb=tpb, tc=tc, tn=tn),
        grid=(n // tm, nout // tn),
        in_specs=[
            pl.BlockSpec((tm, d), lambda i, j: (i, 0)),
            pl.BlockSpec((1, d), lambda i, j: (0, 0)),
            *_mod_specs(d, tpb, nb, (0, 1)),
            pl.BlockSpec((d, tn), lambda i, j: (0, j)),
        ],
        out_specs=[
            pl.BlockSpec((tm, tn), lambda i, j: (i, j)),
            pl.BlockSpec((tm, P_SIDE), lambda i, j: (i, 0)),
        ],
        out_shape=[jax.ShapeDtypeStruct((n, nout), BF16), jax.ShapeDtypeStruct((n, P_SIDE), F32)],
        scratch_shapes=[pltpu.VMEM((tm, d), BF16)],
        compiler_params=_cparams(("arbitrary", "arbitrary")),
        name="in_proj",
    )(xa, gain, mod, mod, mod, mod, w)


def _mla_prep_kernel(cq_ref, ckv_ref, kr_ref, cos_ref, sin_ref, gq_ref, gkv_ref, wq_ref, wk_ref, wvt_ref,
                     q_ref, k_ref, vt_ref):
    cos = cos_ref[...]
    sin = sin_ref[...]
    quarter = MLA_ROPE_DIM // 4

    def rope(x):
        return x * cos + _rot_half(x, quarter) * sin

    qn = _rms(cq_ref[...].astype(F32), gq_ref[...]).astype(BF16)
    q = jnp.dot(qn, wq_ref[...], preferred_element_type=F32)
    kvn = _rms(ckv_ref[...].astype(F32), gkv_ref[...]).astype(BF16)
    kn = jnp.dot(kvn, wk_ref[...], preferred_element_type=F32)
    vt = lax.dot_general(wvt_ref[...], kvn, (((1,), (1,)), ((), ())), preferred_element_type=F32)
    vt_ref[...] = vt.astype(BF16)
    kr = rope(kr_ref[...]).astype(BF16)
    for h in range(MLA_HEADS):
        a = h * MLA_QK_PAD
        q_ref[:, a:a + LANES] = (q[:, a:a + LANES] * (MLA_SCALE * LOG2E)).astype(BF16)
        q_ref[:, a + LANES:a + 2 * LANES] = (rope(q[:, a + LANES:a + 2 * LANES])
                                             * (MLA_SCALE * LOG2E)).astype(BF16)
        k_ref[:, a:a + LANES] = kn[:, h * LANES:(h + 1) * LANES].astype(BF16)
        k_ref[:, a + LANES:a + 2 * LANES] = kr


def _mla_prep(proj, side, cos_t, sin_t, g_q, g_kv, wq, wk, wvt, *, s):
    n = proj.shape[0]
    tm = _row_tile(s, 768)
    tpb = s // tm
    hq = MLA_HEADS * MLA_QK_PAD
    hv = MLA_HEADS * MLA_V_DIM
    full = lambda a: pl.BlockSpec(a.shape, lambda i: (0,) * a.ndim)
    return pl.pallas_call(
        _mla_prep_kernel,
        grid=(n // tm,),
        in_specs=[
            pl.BlockSpec((tm, MLA_Q_LORA), lambda i: (i, P_CQ // MLA_Q_LORA)),
            pl.BlockSpec((tm, MLA_KV_LORA), lambda i: (i, P_CKV // MLA_KV_LORA)),
            pl.BlockSpec((tm, LANES), lambda i: (i, 0)),
            pl.BlockSpec((tm, LANES), lambda i: (i % tpb, 0)),
            pl.BlockSpec((tm, LANES), lambda i: (i % tpb, 0)),
            full(g_q), full(g_kv), full(wq), full(wk), full(wvt),
        ],
        out_specs=[
            pl.BlockSpec((tm, hq), lambda i: (i, 0)),
            pl.BlockSpec((tm, hq), lambda i: (i, 0)),
            pl.BlockSpec((hv, tm), lambda i: (0, i)),
        ],
        out_shape=[
            jax.ShapeDtypeStruct((n, hq), BF16),
            jax.ShapeDtypeStruct((n, hq), BF16),
            jax.ShapeDtypeStruct((hv, n), BF16),
        ],
        compiler_params=_cparams(("arbitrary",)),
        name="mla_prep",
    )(proj, proj, side, cos_t, sin_t, g_q, g_kv, wq, wk, wvt)


def _softplus(x):
    return jnp.maximum(x, 0.0) + jnp.log1p(jnp.exp(-jnp.abs(x)))


def _aux_prep_kernel(dq_ref, dk_ref, dv_ref, ab_ref, cos_ref, sin_ref, alog_ref, dtb_ref,
                     q_ref, k_ref, vt_ref, gates_ref, gt_ref, *, tm):
    cos = cos_ref[...]
    sin = sin_ref[...]
    quarter = DIFF_QK_DIM // 4
    for h in range(DIFF_HEADS):
        sl = slice(h * LANES, (h + 1) * LANES)
        xq = dq_ref[:, sl].astype(F32)
        xk = dk_ref[:, sl].astype(F32)
        q_ref[:, sl] = ((xq * cos + _rot_half(xq, quarter) * sin) * (DIFF_SCALE * LOG2E)).astype(BF16)
        k_ref[:, sl] = (xk * cos + _rot_half(xk, quarter) * sin).astype(BF16)
    vt_ref[...] = dv_ref[...].astype(F32).T.astype(BF16)

    ab = ab_ref[...]
    g = -jnp.exp(alog_ref[...]) * _softplus(ab + dtb_ref[...])
    beta = jax.nn.sigmoid(ab)
    row = lax.broadcasted_iota(jnp.int32, (tm, LANES), 0) % GDN_CHUNK
    pre = g
    suf = g
    step = 1
    while step < GDN_CHUNK:
        pre = pre + jnp.where(row >= step, pltpu.roll(pre, step, 0), 0.0)
        suf = suf + jnp.where(row < GDN_CHUNK - step, pltpu.roll(suf, tm - step, 0), 0.0)
        step *= 2
    lane = lax.broadcasted_iota(jnp.int32, (tm, LANES), 1)
    total = pltpu.roll(pre + suf - g, 4 * GDN_HEADS, 1)
    gates = jnp.where(lane < GDN_HEADS, pre,
                      jnp.where(lane < 2 * GDN_HEADS, suf, jnp.where(lane < 4 * GDN_HEADS, beta, total)))
    gates_ref[...] = gates
    gt_ref[...] = gates.T[0:gt_ref.shape[0], :]


def _aux_prep(proj, side, cos_t, sin_t, alog_row, dtb_row, *, s):
    n = proj.shape[0]
    tm = _row_tile(s, 768)
    tpb = s // tm
    wq = DIFF_HEADS * 2 * DIFF_QK_DIM
    wv = DIFF_HEADS * DIFF_V_DIM
    gt_rows = 32
    return pl.pallas_call(
        functools.partial(_aux_prep_kernel, tm=tm),
        grid=(n // tm,),
        in_specs=[
            pl.BlockSpec((tm, wq), lambda i: (i, P_DQ // wq)),
            pl.BlockSpec((tm, wq), lambda i: (i, P_DK // wq)),
            pl.BlockSpec((tm, wv), lambda i: (i, P_DV // wv)),
            pl.BlockSpec((tm, LANES), lambda i: (i, (P_GAB - P_KR) // LANES)),
            pl.BlockSpec((tm, LANES), lambda i: (i % tpb, 0)),
            pl.BlockSpec((tm, LANES), lambda i: (i % tpb, 0)),
            pl.BlockSpec((1, LANES), lambda i: (0, 0)),
            pl.BlockSpec((1, LANES), lambda i: (0, 0)),
        ],
        out_specs=[
            pl.BlockSpec((tm, wq), lambda i: (i, 0)),
            pl.BlockSpec((tm, wq), lambda i: (i, 0)),
            pl.BlockSpec((wv, tm), lambda i: (0, i)),
            pl.BlockSpec((tm, LANES), lambda i: (i, 0)),
            pl.BlockSpec((gt_rows, tm), lambda i: (0, i)),
        ],
        out_shape=[
            jax.ShapeDtypeStruct((n, wq), BF16),
            jax.ShapeDtypeStruct((n, wq), BF16),
            jax.ShapeDtypeStruct((wv, n), BF16),
            jax.ShapeDtypeStruct((n, LANES), F32),
            jax.ShapeDtypeStruct((gt_rows, n), F32),
        ],
        compiler_params=_cparams(("arbitrary",)),
        name="aux_prep",
    )(proj, proj, proj, side, cos_t, sin_t, alog_row, dtb_row)


def _gdn_prep_kernel(x_ref, w_ref, o_ref, *, s, tc, groups):
    c = pl.program_id(1)
    x = x_ref[...].astype(F32)
    w = w_ref[...]
    t = lax.broadcasted_iota(jnp.int32, (s, 1), 0)
    lo = jnp.where(t < tc, 0, tc)
    hi = jnp.where(t < tc, tc, s)
    half = GDN_CONV // 2
    acc = x * w[half:half + 1, :]
    for off in range(-half, half + 1):
        if off == 0:
            continue
        xs = pltpu.roll(x, (-off) % s, 0)
        ok = (t + off >= lo) & (t + off < hi)
        acc = acc + jnp.where(ok, xs, 0.0) * w[off + half:off + half + 1, :]
    y = _silu(acc)
    qscale = jnp.where(c < groups, GDN_K_DIM ** -0.5, 1.0)
    is_qk = c < 2 * groups
    for gi in range(x.shape[1] // LANES):
        sl = slice(gi * LANES, (gi + 1) * LANES)
        yg = y[:, sl]
        nrm = yg * lax.rsqrt(jnp.sum(yg * yg, axis=-1, keepdims=True) + NORM_EPS) * qscale
        o_ref[:, sl] = jnp.where(is_qk, nrm, yg)


def _gdn_prep(proj, conv_w, *, s, tc, nb):
    n = proj.shape[0]
    cw = 3 * LANES
    groups = GDN_HEADS * GDN_K_DIM // cw
    off = P_GQKV // cw
    return pl.pallas_call(
        functools.partial(_gdn_prep_kernel, s=s, tc=tc, groups=groups),
        grid=(nb, GQKV_W // cw),
        in_specs=[
            pl.BlockSpec((s, cw), lambda b, c: (b, off + c)),
            pl.BlockSpec((GDN_CONV, cw), lambda b, c: (0, c)),
        ],
        out_specs=pl.BlockSpec((s, cw), lambda b, c: (b, c)),
        out_shape=jax.ShapeDtypeStruct((n, GQKV_W), F32),
        compiler_params=_cparams(("arbitrary", "arbitrary")),
        name="gdn_prep",
    )(proj, conv_w)


def _gdn_solve_kernel(q_ref, k_ref, v_ref, gates_ref, gt_ref, u_ref, w_ref, qg_ref, kd_ref, in_ref,
                      l_scr, x_scr, rhs_scr):
    t = q_ref.shape[0]
    c = GDN_CHUNK
    nh = GDN_HEADS
    ii = lax.broadcasted_iota(jnp.int32, (t, t), 0)
    jj = lax.broadcasted_iota(jnp.int32, (t, t), 1)
    log2c = c.bit_length() - 1
    same = (ii >> log2c) == (jj >> log2c)
    eye = jnp.where(ii == jj, 1.0, 0.0)
    level_masks = [((ii >> (sh + 1)) == (jj >> (sh + 1))) & ((ii >> sh) != (jj >> sh))
                   for sh in range(1, log2c)]
    pair = (ii >> 1) == (jj >> 1)
    gates = gates_ref[...]
    gt = gt_ref[...]
    nt = (((1,), (1,)), ((), ()))
    for h in range(nh):
        sl = slice(h * LANES, (h + 1) * LANES)
        q = q_ref[:, sl]
        k = k_ref[:, sl]
        v = v_ref[:, sl]
        kb = k.astype(BF16)
        kk = lax.dot_general(kb, kb, nt, preferred_element_type=F32)
        qk = lax.dot_general(q.astype(BF16), kb, nt, preferred_element_type=F32)
        for d in range(2):
            fwd = d == 0
            gi = d * nh + h
            gcol = gates[:, gi:gi + 1]
            bcol = gates[:, 2 * nh + gi:2 * nh + gi + 1]
            glcol = gates[:, 4 * nh + gi:4 * nh + gi + 1]
            grow = gt[gi:gi + 1, :]
            incl = same & ((ii >= jj) if fwd else (ii <= jj))
            strict = same & ((ii > jj) if fwd else (ii < jj))
            decay = jnp.where(incl, jnp.exp(jnp.where(incl, gcol - grow, 0.0)), 0.0)
            lmat = jnp.where(strict, bcol * kk * decay, 0.0)
            l_scr[gi] = lmat.astype(BF16)
            x_scr[gi] = (eye - jnp.where(pair, lmat, 0.0)).astype(BF16)
            eg = jnp.exp(gcol)
            rhs_scr[gi] = jnp.concatenate([v * bcol, k * (bcol * eg)], axis=1).astype(BF16)
            cs = slice(gi * LANES, (gi + 1) * LANES)
            qg_ref[:, cs] = (q * eg).astype(BF16)
            kd_ref[:, cs] = (k * jnp.exp(glcol - gcol)).astype(BF16)
            intra = qk * decay
            compact = jnp.concatenate([intra[n * c:(n + 1) * c, n * c:(n + 1) * c] for n in range(t // c)],
                                      axis=0)
            in_ref[:, cs] = jnp.concatenate([compact, jnp.zeros((t, LANES - c), F32)], axis=1).astype(BF16)
    zero = jnp.zeros((t, t), BF16)
    for m in level_masks:
        for gi in range(2 * nh):
            xb = x_scr[gi]
            nx = jnp.dot(jnp.where(m, l_scr[gi], zero), xb, preferred_element_type=F32)
            z = jnp.dot(xb, nx.astype(BF16), preferred_element_type=F32)
            x_scr[gi] = xb - z.astype(BF16)
    for gi in range(2 * nh):
        uw = jnp.dot(x_scr[gi], rhs_scr[gi], preferred_element_type=F32)
        cs = slice(gi * LANES, (gi + 1) * LANES)
        u_ref[:, cs] = uw[:, :GDN_V_DIM]
        w_ref[:, cs] = uw[:, GDN_V_DIM:].astype(BF16)


def _gdn_solve(qkv, gates, gt, *, s, nb):
    n = qkv.shape[0]
    t = GDN_TILE
    nt = s // t
    hw = GDN_HEADS * GDN_K_DIM
    cw = 2 * hw
    row = lambda b, i: (b * nt + i, 0)
    return pl.pallas_call(
        _gdn_solve_kernel,
        grid=(nb, nt),
        in_specs=[
            pl.BlockSpec((t, hw), lambda b, i: (b * nt + i, 0)),
            pl.BlockSpec((t, hw), lambda b, i: (b * nt + i, 1)),
            pl.BlockSpec((t, hw), lambda b, i: (b * nt + i, 2)),
            pl.BlockSpec((t, LANES), row),
            pl.BlockSpec((gt.shape[0], t), lambda b, i: (0, b * nt + i)),
        ],
        out_specs=[pl.BlockSpec((t, cw), row)] * 5,
        out_shape=[jax.ShapeDtypeStruct((n, cw), F32)] + [jax.ShapeDtypeStruct((n, cw), BF16)] * 4,
        scratch_shapes=[
            pltpu.VMEM((2 * GDN_HEADS, t, t), BF16),
            pltpu.VMEM((2 * GDN_HEADS, t, t), BF16),
            pltpu.VMEM((2 * GDN_HEADS, t, GDN_V_DIM + GDN_K_DIM), BF16),
        ],
        compiler_params=_cparams(("arbitrary", "arbitrary")),
        name="gdn_solve",
    )(qkv, qkv, qkv, gates, gt)


def _gdn_scan_kernel(uf, ub, wf, wb, qgf, qgb, kdf, kdb, inf, inb, gf_ref, gb_ref, of_ref, ob_ref, s_ref):
    @pl.when(pl.program_id(1) == 0)
    def _():
        s_ref[...] = jnp.zeros_like(s_ref)

    c = GDN_CHUNK
    nh = GDN_HEADS
    npt = uf.shape[0] // c
    tn = (((0,), (0,)), ((), ()))
    dirs = ((uf, wf, qgf, kdf, inf, gf_ref, of_ref), (ub, wb, qgb, kdb, inb, gb_ref, ob_ref))
    for p in range(npt):
        chains = []
        for d, (u_ref, w_ref, qg_ref, kd_ref, in_ref, g_ref, o_ref) in enumerate(dirs):
            pc = p if d == 0 else npt - 1 - p
            rs = slice(pc * c, (pc + 1) * c)
            totals = jnp.exp(g_ref[pc * c:pc * c + 1, :])
            for h in range(nh):
                j = d * nh + h
                cs = slice(h * LANES, (h + 1) * LANES)
                chains.append((j, rs, cs, u_ref, w_ref, qg_ref, kd_ref, in_ref, o_ref,
                               totals[:, 4 * nh + j:4 * nh + j + 1]))
        states = [s_ref[ch[0]] for ch in chains]
        prods = []
        for st, (j, rs, cs, u_ref, w_ref, qg_ref, kd_ref, in_ref, o_ref, decay) in zip(states, chains):
            wq = jnp.concatenate([w_ref[rs, cs], qg_ref[rs, cs]], axis=0)
            prods.append(jnp.dot(wq, st.astype(BF16), preferred_element_type=F32))
        v_news = [(ch[3][ch[1], ch[2]] - r[:c]).astype(BF16) for r, ch in zip(prods, chains)]
        for st, r, v_new, (j, rs, cs, u_ref, w_ref, qg_ref, kd_ref, in_ref, o_ref, decay) in zip(
                states, prods, v_news, chains):
            intra = in_ref[rs, cs][:, :c]
            o_ref[rs, cs] = r[c:] + jnp.dot(intra, v_new, preferred_element_type=F32)
            s_ref[j] = st * decay + lax.dot_general(kd_ref[rs, cs], v_new, tn, preferred_element_type=F32)


def _gdn_scan(u, w, qg, kd, intra, gates, *, s, tc, nb):
    n = u.shape[0]
    t = GDN_TILE
    nt = s // t
    nct = tc // t
    hw = GDN_HEADS * GDN_V_DIM

    def bwd_tile(i):
        return jnp.where(i < nct, nct - 1 - i, nct + nt - 1 - i)

    fwd = lambda b, i: (b * nt + i, 0)
    bwd = lambda b, i: (b * nt + bwd_tile(i), 1)
    pair = [pl.BlockSpec((t, hw), fwd), pl.BlockSpec((t, hw), bwd)]
    return pl.pallas_call(
        _gdn_scan_kernel,
        grid=(nb, nt),
        in_specs=pair * 5 + [
            pl.BlockSpec((t, LANES), fwd),
            pl.BlockSpec((t, LANES), lambda b, i: (b * nt + bwd_tile(i), 0)),
        ],
        out_specs=[
            pl.BlockSpec((t, hw), fwd),
            pl.BlockSpec((t, hw), lambda b, i: (b * nt + bwd_tile(i), 0)),
        ],
        out_shape=[jax.ShapeDtypeStruct((n, hw), F32)] * 2,
        scratch_shapes=[pltpu.VMEM((2 * GDN_HEADS, GDN_K_DIM, GDN_V_DIM), F32)],
        compiler_params=_cparams(("arbitrary", "arbitrary")),
        name="gdn_scan",
    )(u, u, w, w, qg, qg, kd, kd, intra, intra, gates, gates)


def _gdn_out_kernel(of_ref, ob_ref, z_ref, gain_ref, o_ref):
    gain = gain_ref[...]
    for h in range(GDN_HEADS):
        sl = slice(h * LANES, (h + 1) * LANES)
        o = of_ref[:, sl] + ob_ref[:, sl]
        o_ref[:, sl] = (_rms(o, gain) * _silu(z_ref[:, sl].astype(F32))).astype(o_ref.dtype)


def _gdn_out(o_f, o_b, proj, gain, *, s):
    n, hw = o_f.shape
    tm = _row_tile(s, 768)
    return pl.pallas_call(
        _gdn_out_kernel,
        grid=(n // tm,),
        in_specs=[
            pl.BlockSpec((tm, hw), lambda i: (i, 0)),
            pl.BlockSpec((tm, hw), lambda i: (i, 0)),
            pl.BlockSpec((tm, hw), lambda i: (i, P_GZ // hw)),
            pl.BlockSpec((1, LANES), lambda i: (0, 0)),
        ],
        out_specs=pl.BlockSpec((tm, hw), lambda i: (i, 0)),
        out_shape=jax.ShapeDtypeStruct((n, hw), BF16),
        compiler_params=_cparams(("arbitrary",)),
        name="gdn_out",
    )(o_f, o_b, proj, gain)


def _softmax2_t(st):
    m = jnp.max(st, axis=0, keepdims=True)
    e = jnp.exp2(st - m)
    return e, 1.0 / jnp.sum(e, axis=0, keepdims=True)


def _pipelined_query_tiles(nq, scores, finish):
    for t in range(min(ATTN_AHEAD, nq)):
        scores(t)
    for t in range(nq):
        if t + ATTN_AHEAD < nq:
            scores(t + ATTN_AHEAD)
        finish(t)


def _mla_attn_kernel(q_ref, k_ref, vt_ref, o_ref, s_scr, *, tc, tq):
    nt = (((1,), (1,)), ((), ()))
    s = q_ref.shape[0]
    keys = lambda t: tc if t * tq < tc else s

    def scores(t):
        kn = keys(t)
        s_scr[t % ATTN_BUFS, 0:kn, :] = lax.dot_general(k_ref[0:kn, :], q_ref[t * tq:(t + 1) * tq, :], nt,
                                                preferred_element_type=F32)

    def finish(t):
        kn = keys(t)
        e, rinv = _softmax2_t(s_scr[t % ATTN_BUFS, 0:kn, :])
        ot = jnp.dot(vt_ref[:, 0:kn], e.astype(BF16), preferred_element_type=F32) * rinv
        o_ref[t * tq:(t + 1) * tq, :] = ot.T.astype(o_ref.dtype)

    _pipelined_query_tiles(s // tq, scores, finish)


def _mla_attn(q, k, vt, *, s, tc, nb):
    n = q.shape[0]
    tq = _row_tile(tc, 256)
    return pl.pallas_call(
        functools.partial(_mla_attn_kernel, tc=tc, tq=tq),
        grid=(nb, MLA_HEADS),
        in_specs=[
            pl.BlockSpec((s, MLA_QK_PAD), lambda b, h: (b, h)),
            pl.BlockSpec((s, MLA_QK_PAD), lambda b, h: (b, h)),
            pl.BlockSpec((MLA_V_DIM, s), lambda b, h: (h, b)),
        ],
        out_specs=pl.BlockSpec((s, MLA_V_DIM), lambda b, h: (b, h)),
        out_shape=jax.ShapeDtypeStruct((n, MLA_HEADS * MLA_V_DIM), BF16),
        scratch_shapes=[pltpu.VMEM((ATTN_BUFS, s, tq), F32)],
        compiler_params=_cparams(("arbitrary", "arbitrary")),
        name="mla_attn",
    )(q, k, vt)


def _diff_attn_kernel(q_ref, k_ref, vt_ref, lam_ref, gain_ref, o_ref, s_scr, *, tc, tq, lam_init):
    nt = (((1,), (1,)), ((), ()))
    s = q_ref.shape[0]
    keys = lambda t: tc if t * tq < tc else s
    lp = lam_ref[...]
    lam = (jnp.exp(jnp.sum(lp[0:1] * lp[1:2], axis=-1, keepdims=True))
           - jnp.exp(jnp.sum(lp[2:3] * lp[3:4], axis=-1, keepdims=True)) + lam_init)
    gain = gain_ref[...] * (1.0 - lam_init)
    lane = lax.broadcasted_iota(jnp.int32, (tq, LANES), 1)

    def scores(t):
        kn = keys(t)
        q = q_ref[t * tq:(t + 1) * tq, :]
        zero = jnp.zeros_like(q)
        for g in range(2):
            qg = jnp.where((lane < DIFF_QK_DIM) == (g == 0), q, zero)
            s_scr[2 * (t % ATTN_BUFS) + g, 0:kn, :] = lax.dot_general(k_ref[0:kn, :], qg, nt,
                                                              preferred_element_type=F32)

    def finish(t):
        kn = keys(t)
        e1, r1 = _softmax2_t(s_scr[2 * (t % ATTN_BUFS), 0:kn, :])
        e2, r2 = _softmax2_t(s_scr[2 * (t % ATTN_BUFS) + 1, 0:kn, :])
        p = e1 - e2 * (r2 * lam / r1)
        ot = jnp.dot(vt_ref[:, 0:kn], p.astype(BF16), preferred_element_type=F32) * r1
        ms = jnp.mean(ot * ot, axis=0, keepdims=True)
        ot = ot * lax.rsqrt(ms + NORM_EPS) * gain
        o_ref[t * tq:(t + 1) * tq, :] = ot.T.astype(o_ref.dtype)

    _pipelined_query_tiles(s // tq, scores, finish)


def _diff_attn(q, k, vt, lam_p, gain_col, *, s, tc, nb, lam_init):
    n = q.shape[0]
    tq = _row_tile(tc, 256)
    return pl.pallas_call(
        functools.partial(_diff_attn_kernel, tc=tc, tq=tq, lam_init=lam_init),
        grid=(nb, DIFF_HEADS),
        in_specs=[
            pl.BlockSpec((s, LANES), lambda b, h: (b, h)),
            pl.BlockSpec((s, LANES), lambda b, h: (b, h)),
            pl.BlockSpec((DIFF_V_DIM, s), lambda b, h: (h, b)),
            pl.BlockSpec(lam_p.shape, lambda b, h: (0, 0)),
            pl.BlockSpec(gain_col.shape, lambda b, h: (0, 0)),
        ],
        out_specs=pl.BlockSpec((s, DIFF_V_DIM), lambda b, h: (b, h)),
        out_shape=jax.ShapeDtypeStruct((n, DIFF_HEADS * DIFF_V_DIM), BF16),
        scratch_shapes=[pltpu.VMEM((2 * ATTN_BUFS, s, tq), F32)],
        compiler_params=_cparams(("arbitrary", "arbitrary")),
        name="diff_attn",
    )(q, k, vt, lam_p, gain_col)


def _out_proj_kernel(x_ref, a1_ref, a2_ref, a3_ref, w1_ref, w2_ref, w3_ref, gl_ref, gc_ref, o_ref,
                     *, tm, tpb, tc):
    i = pl.program_id(0)
    acc = jnp.dot(a1_ref[...], w1_ref[...], preferred_element_type=F32)
    acc += jnp.dot(a2_ref[...], w2_ref[...], preferred_element_type=F32)
    acc += jnp.dot(a3_ref[...], w3_ref[...], preferred_element_type=F32)
    gate = jnp.where(_ctx_rows(i, tpb, tm, tc), gc_ref[...], gl_ref[...])
    o_ref[...] = x_ref[...] + gate * acc


def _out_proj(xa, a1, a2, a3, w, mod, *, s, tc, nb):
    n, d = xa.shape
    tm = _row_tile(s, 1152)
    tn = _col_tile(d, 512)
    tpb = s // tm
    k1, k2, k3 = a1.shape[1], a2.shape[1], a3.shape[1]
    assert k1 == k2 and (k1 + k2) % k3 == 0
    return pl.pallas_call(
        functools.partial(_out_proj_kernel, tm=tm, tpb=tpb, tc=tc),
        grid=(n // tm, d // tn),
        in_specs=[
            pl.BlockSpec((tm, tn), lambda i, j: (i, j)),
            pl.BlockSpec((tm, k1), lambda i, j: (i, 0)),
            pl.BlockSpec((tm, k2), lambda i, j: (i, 0)),
            pl.BlockSpec((tm, k3), lambda i, j: (i, 0)),
            pl.BlockSpec((k1, tn), lambda i, j: (0, j)),
            pl.BlockSpec((k2, tn), lambda i, j: (1, j)),
            pl.BlockSpec((k3, tn), lambda i, j: ((k1 + k2) // k3, j)),
            pl.BlockSpec((None, None, 1, tn), lambda i, j: (i // tpb, 2, 0, j)),
            pl.BlockSpec((None, None, 1, tn), lambda i, j: (nb, 2, 0, j)),
        ],
        out_specs=pl.BlockSpec((tm, tn), lambda i, j: (i, j)),
        out_shape=jax.ShapeDtypeStruct((n, d), F32),
        compiler_params=_cparams(("arbitrary", "arbitrary")),
        name="out_proj",
    )(xa, a1, a2, a3, w, w, w, mod, mod)


def _mlp_kernel(x_ref, g_ref, shl, scl, shc, scc, gl_ref, gc_ref, w1_ref, w2_ref, o_ref, hn_ref,
                *, tm, tpb, tc):
    i = pl.program_id(0)
    j = pl.program_id(1)
    first_row = (i % tpb) * tm

    @pl.when(j == 0)
    def _():
        _norm_mod_rows(x_ref, g_ref, shl, scl, shc, scc, hn_ref, first_row, tc)
        o_ref[...] = jnp.zeros_like(o_ref)

    tf = w1_ref.shape[1]
    d = o_ref.shape[1]
    hn = hn_ref[...]
    cf = math.gcd(tf, MLP_CHUNK)
    cd = math.gcd(d, MLP_CHUNK)
    acts = []
    for c0 in range(0, tf, cf):
        a = jnp.dot(hn, w1_ref[:, c0:c0 + cf], preferred_element_type=F32)
        acts.append(jnp.square(jnp.maximum(a, 0.0)).astype(BF16))
    act = jnp.concatenate(acts, axis=1) if len(acts) > 1 else acts[0]
    for n0 in range(0, d, cd):
        o_ref[:, n0:n0 + cd] += jnp.dot(act, w2_ref[:, n0:n0 + cd], preferred_element_type=F32)

    @pl.when(j == pl.num_programs(1) - 1)
    def _():
        _gated_residual_rows(x_ref, o_ref, gl_ref, gc_ref, first_row, tc)


def _mlp(xa, gain, mod, w1, w2, *, s, tc, nb, n_rows):
    d = xa.shape[1]
    dff = w1.shape[1]
    tm = _row_tile(s, 768)
    tf = _col_tile(dff, 1024)
    tpb = s // tm
    mod_gate = lambda row: pl.BlockSpec((None, None, 1, d), row)
    return pl.pallas_call(
        functools.partial(_mlp_kernel, tm=tm, tpb=tpb, tc=tc),
        grid=(n_rows // tm, dff // tf),
        in_specs=[
            pl.BlockSpec((tm, d), lambda i, j: (i, 0)),
            pl.BlockSpec((1, d), lambda i, j: (0, 0)),
            *_mod_specs(d, tpb, nb, (3, 4)),
            mod_gate(lambda i, j: (i // tpb, 5, 0, 0)),
            mod_gate(lambda i, j: (nb, 5, 0, 0)),
            pl.BlockSpec((d, tf), lambda i, j: (0, j)),
            pl.BlockSpec((tf, d), lambda i, j: (j, 0)),
        ],
        out_specs=pl.BlockSpec((tm, d), lambda i, j: (i, 0)),
        out_shape=jax.ShapeDtypeStruct((n_rows, d), F32),
        scratch_shapes=[pltpu.VMEM((tm, d), BF16)],
        compiler_params=_cparams(("arbitrary", "arbitrary")),
        name="mlp",
    )(xa, gain, mod, mod, mod, mod, mod, mod, w1, w2)


def _final_norm_kernel(x_ref, g_ref, o_ref):
    o_ref[...] = _rms(x_ref[...], g_ref[...])


def _final_norm(xa, gain, *, s, tc, nb):
    d = xa.shape[1]
    t = s - tc
    tr = _row_tile(math.gcd(tc, t), 512)
    per_b = s // tr
    skip = tc // tr
    nt = t // tr
    return pl.pallas_call(
        _final_norm_kernel,
        grid=(nb, nt),
        in_specs=[
            pl.BlockSpec((tr, d), lambda b, r: (b * per_b + skip + r, 0)),
            pl.BlockSpec((1, d), lambda b, r: (0, 0)),
        ],
        out_specs=pl.BlockSpec((tr, d), lambda b, r: (b * nt + r, 0)),
        out_shape=jax.ShapeDtypeStruct((nb * t, d), F32),
        compiler_params=_cparams(("arbitrary", "arbitrary")),
        name="final_norm",
    )(xa, gain)


def _rope_tables(t, tc, dim, reps):
    rows = t // GRID_W
    row_pos = jnp.repeat(jnp.arange(rows, dtype=F32), GRID_W)
    col_pos = jnp.tile(jnp.arange(GRID_W, dtype=F32), rows)
    quarter = dim // 4
    inv_freq = ROPE_BASE ** (-jnp.arange(quarter, dtype=F32) / quarter)
    ang_r = row_pos[:, None] * inv_freq[None, :]
    ang_c = col_pos[:, None] * inv_freq[None, :]
    ang = jnp.concatenate([ang_r, ang_r, ang_c, ang_c], axis=-1)
    cos = jnp.tile(jnp.cos(ang), (1, reps))
    sin = jnp.tile(jnp.sin(ang), (1, reps))
    pad = LANES - cos.shape[1]
    cos = jnp.pad(cos, ((0, 0), (0, pad)), constant_values=1.0)
    sin = jnp.pad(sin, ((0, 0), (0, pad)))
    cos = jnp.concatenate([jnp.ones((tc, LANES), F32), cos], axis=0)
    sin = jnp.concatenate([jnp.zeros((tc, LANES), F32), sin], axis=0)
    return cos, sin


def _permute_w_in(w):
    d = w.shape[0]
    sizes = (MLA_Q_LORA, MLA_KV_LORA, MLA_ROPE_DIM, GDN_HEADS * GDN_K_DIM, GDN_HEADS * GDN_K_DIM,
             GDN_HEADS * GDN_V_DIM, GDN_HEADS * GDN_V_DIM, 4 * GDN_HEADS,
             DIFF_HEADS * 2 * DIFF_QK_DIM, DIFF_HEADS * 2 * DIFF_QK_DIM, DIFF_HEADS * DIFF_V_DIM)
    offs = [0]
    for sz in sizes:
        offs.append(offs[-1] + sz)
    cq, ckv, kr, gq, gk, gv, gz, gab, dq, dk, dv = (w[:, offs[i]:offs[i + 1]] for i in range(len(sizes)))
    zpad = lambda a: jnp.pad(a, ((0, 0), (0, LANES - a.shape[1])))
    out = jnp.concatenate([cq, gq, gk, gv, gz, zpad(kr), zpad(gab), ckv, dq, dk, dv], axis=1)
    assert out.shape == (d, P_TOTAL)
    return out.astype(BF16)


def kernel(x, c, ctx, c_ctx, w_ada, b_ada, norm1_g, norm2_g, w_in, mla_q_norm_g, mla_kv_norm_g, mla_w_uq, mla_w_ukv, gdn_conv_w, gdn_a_log, gdn_dt_bias, gdn_norm_g, diff_lambda, diff_norm_g, w_out, w_mlp1, w_mlp2, final_norm_g):
    nb, t, d = x.shape
    tc = ctx.shape[1]
    depth = w_ada.shape[0]
    s = tc + t
    n = nb * s
    assert t % GRID_W == 0 and t % GDN_TILE == 0 and tc % GDN_TILE == 0 and tc % NORM_ROWS == 0

    xa = jnp.concatenate([ctx, x], axis=1).reshape(n, d)

    r = -(-(nb + 1) // 8) * 8
    s_rows = jnp.concatenate([c, c_ctx[None, :], jnp.zeros((r - nb - 1, d), F32)], axis=0)
    mod_all = _adaln(s_rows, w_ada, b_ada).reshape(depth, r, 6, 1, d)

    cos_m, sin_m = _rope_tables(t, tc, MLA_ROPE_DIM, 1)
    cos_d, sin_d = _rope_tables(t, tc, DIFF_QK_DIM, 2)

    for l in range(depth):
        mod = mod_all[l]
        lam_init = 0.8 - 0.6 * math.exp(-0.3 * l)

        w_in_p = _permute_w_in(w_in[l])
        wq = mla_w_uq[l].reshape(MLA_Q_LORA, MLA_HEADS, MLA_NOPE_DIM + MLA_ROPE_DIM)
        wq = jnp.pad(wq, ((0, 0), (0, 0), (0, MLA_QK_PAD - wq.shape[2])))
        wq = wq.reshape(MLA_Q_LORA, MLA_HEADS * MLA_QK_PAD).astype(BF16)
        wkv = mla_w_ukv[l].reshape(MLA_KV_LORA, MLA_HEADS, MLA_NOPE_DIM + MLA_V_DIM)
        wk = wkv[:, :, :MLA_NOPE_DIM].reshape(MLA_KV_LORA, MLA_HEADS * MLA_NOPE_DIM).astype(BF16)
        wvt = wkv[:, :, MLA_NOPE_DIM:].reshape(MLA_KV_LORA, MLA_HEADS * MLA_V_DIM).T.astype(BF16)
        gate_pad = LANES - 2 * GDN_HEADS
        alog_row = jnp.pad(gdn_a_log[l].reshape(1, 2 * GDN_HEADS), ((0, 0), (0, gate_pad)))
        dtb_row = jnp.pad(gdn_dt_bias[l].reshape(1, 2 * GDN_HEADS), ((0, 0), (0, gate_pad)))

        proj, side = _in_proj(xa, norm1_g[l][None, :], mod, w_in_p, s=s, tc=tc, nb=nb)

        mq, mk, mvt = _mla_prep(proj, side, cos_m, sin_m, mla_q_norm_g[l][None, :],
                                mla_kv_norm_g[l][None, :], wq, wk, wvt, s=s)
        mla_o = _mla_attn(mq, mk, mvt, s=s, tc=tc, nb=nb)

        dq, dk, dvt, gates, gt = _aux_prep(proj, side, cos_d, sin_d, alog_row, dtb_row, s=s)
        diff_o = _diff_attn(dq, dk, dvt, diff_lambda[l], diff_norm_g[l][:, None], s=s, tc=tc, nb=nb,
                            lam_init=lam_init)

        qkv = _gdn_prep(proj, gdn_conv_w[l], s=s, tc=tc, nb=nb)
        gu, gw, gqg, gkd, gin = _gdn_solve(qkv, gates, gt, s=s, nb=nb)
        o_f, o_b = _gdn_scan(gu, gw, gqg, gkd, gin, gates, s=s, tc=tc, nb=nb)
        gdn_o = _gdn_out(o_f, o_b, proj, gdn_norm_g[l][None, :], s=s)

        xa = _out_proj(xa, mla_o, gdn_o, diff_o, w_out[l].astype(BF16), mod, s=s, tc=tc, nb=nb)
        xa = _mlp(xa, norm2_g[l][None, :], mod, w_mlp1[l].astype(BF16), w_mlp2[l].astype(BF16),
                  s=s, tc=tc, nb=nb, n_rows=n)

    out = _final_norm(xa, final_norm_g[None, :], s=s, tc=tc, nb=nb)
    return out.reshape(nb, t, d)
```

```python
import functools
import math

import jax
import jax.numpy as jnp
from jax import lax
from jax.experimental import pallas as pl
from jax.experimental.pallas import tpu as pltpu

F32 = jnp.float32
BF16 = jnp.bfloat16

GRID_W = 64
MLA_HEADS = 6
MLA_Q_LORA = 768
MLA_KV_LORA = 512
MLA_NOPE_DIM = 128
MLA_ROPE_DIM = 64
MLA_V_DIM = 128
MLA_SCALE = (MLA_NOPE_DIM + MLA_ROPE_DIM) ** -0.5
GDN_HEADS = 6
GDN_K_DIM = 128
GDN_V_DIM = 128
GDN_CONV = 5
GDN_CHUNK = 64
DIFF_HEADS = 4
DIFF_QK_DIM = 64
DIFF_V_DIM = 128
DIFF_SCALE = DIFF_QK_DIM ** -0.5
ROPE_BASE = 10000.0
NORM_EPS = 1e-6
LOG2E = math.log2(math.e)

LANES = 128
MLA_QK_PAD = 2 * LANES
VMEM_LIMIT = 56 * 1024 * 1024
NORM_ROWS = 16
NORM_UNROLL = 6
ATTN_AHEAD = 3
ATTN_BUFS = ATTN_AHEAD + 1
MLP_CHUNK = 512
GDN_TILE = 4 * GDN_CHUNK

GQKV_W = 2 * GDN_HEADS * GDN_K_DIM + GDN_HEADS * GDN_V_DIM
P_CQ = 0
P_GQKV = P_CQ + MLA_Q_LORA
P_GZ = P_GQKV + GQKV_W
P_KR = P_GZ + GDN_HEADS * GDN_V_DIM
P_GAB = P_KR + LANES
P_CKV = P_GAB + LANES
P_DQ = P_CKV + MLA_KV_LORA
P_DK = P_DQ + DIFF_HEADS * 2 * DIFF_QK_DIM
P_DV = P_DK + DIFF_HEADS * 2 * DIFF_QK_DIM
P_TOTAL = P_DV + DIFF_HEADS * DIFF_V_DIM
P_SIDE = 2 * LANES


def _cparams(sem):
    return pltpu.CompilerParams(dimension_semantics=sem, vmem_limit_bytes=VMEM_LIMIT)


def _row_tile(s, cap):
    for step in (128, 64):
        best = 0
        for t in range(step, min(s, cap) + 1, step):
            if s % t == 0:
                best = t
        if best:
            return best
    raise ValueError(f"no row tile for {s}")


def _col_tile(n, cap):
    best = 0
    for t in range(LANES, min(n, cap) + 1, LANES):
        if n % t == 0:
            best = t
    return best


def _rms(x, g):
    ms = jnp.mean(x * x, axis=-1, keepdims=True)
    return x * lax.rsqrt(ms + NORM_EPS) * g


def _silu(x):
    return x * jax.nn.sigmoid(x)


def _norm_mod_rows(x_ref, g_ref, shl, scl, shc, scc, hn_ref, first_row, tc):
    tm = x_ref.shape[0]
    rc = NORM_ROWS
    g = g_ref[...]
    gs_l = g * (1.0 + scl[...])
    gs_c = g * (1.0 + scc[...])
    sh_l = shl[...]
    sh_c = shc[...]

    def body(r, carry):
        r0 = pl.multiple_of(r * rc, rc)
        is_ctx = first_row + r0 < tc
        x = x_ref[pl.ds(r0, rc), :]
        inv = lax.rsqrt(jnp.mean(x * x, axis=-1, keepdims=True) + NORM_EPS)
        gs = jnp.where(is_ctx, gs_c, gs_l)
        sh = jnp.where(is_ctx, sh_c, sh_l)
        hn_ref[pl.ds(r0, rc), :] = (x * inv * gs + sh).astype(hn_ref.dtype)
        return carry

    lax.fori_loop(0, tm // rc, body, 0, unroll=NORM_UNROLL)


def _gated_residual_rows(x_ref, o_ref, gl_ref, gc_ref, first_row, tc):
    tm = x_ref.shape[0]
    rc = NORM_ROWS
    g_l = gl_ref[...]
    g_c = gc_ref[...]

    def body(r, carry):
        r0 = pl.multiple_of(r * rc, rc)
        gate = jnp.where(first_row + r0 < tc, g_c, g_l)
        rows = pl.ds(r0, rc)
        o_ref[rows, :] = x_ref[rows, :] + gate * o_ref[rows, :]
        return carry

    lax.fori_loop(0, tm // rc, body, 0, unroll=NORM_UNROLL)


def _ctx_rows(i, tpb, tm, tc):
    rows = (i % tpb) * tm + lax.broadcasted_iota(jnp.int32, (tm, 1), 0)
    return rows < tc


def _rot_half(x, quarter):
    ax = x.ndim - 1
    n = x.shape[ax]
    lane = lax.broadcasted_iota(jnp.int32, x.shape, ax)
    nxt = pltpu.roll(x, n - quarter, ax)
    prv = pltpu.roll(x, quarter, ax)
    return jnp.where((lane % (2 * quarter)) < quarter, -nxt, prv)


def _adaln_kernel(s_ref, w_ref, b_ref, o_ref):
    s = _silu(s_ref[...])
    acc = jnp.dot(s.astype(BF16), w_ref[...].astype(BF16), preferred_element_type=F32)
    o_ref[...] = acc + b_ref[...]


def _adaln(s_rows, w_ada, b_ada):
    depth, d, n6 = w_ada.shape
    r = s_rows.shape[0]
    tn = _col_tile(n6, 1024)
    return pl.pallas_call(
        _adaln_kernel,
        grid=(depth, n6 // tn),
        in_specs=[
            pl.BlockSpec((r, d), lambda l, j: (0, 0)),
            pl.BlockSpec((None, d, tn), lambda l, j: (l, 0, j)),
            pl.BlockSpec((None, 1, tn), lambda l, j: (l, 0, j)),
        ],
        out_specs=pl.BlockSpec((None, r, tn), lambda l, j: (l, 0, j)),
        out_shape=jax.ShapeDtypeStruct((depth, r, n6), F32),
        compiler_params=_cparams(("arbitrary", "arbitrary")),
        name="adaln",
    )(s_rows, w_ada, b_ada.reshape(depth, 1, n6))


def _mod_specs(d, tpb, nb, ks):
    specs = []
    for k in ks:
        specs.append(pl.BlockSpec((None, None, 1, d), lambda i, j, k=k: (i // tpb, k, 0, 0)))
    for k in ks:
        specs.append(pl.BlockSpec((None, None, 1, d), lambda i, j, k=k: (nb, k, 0, 0)))
    return specs


def _in_proj_kernel(x_ref, g_ref, shl, scl, shc, scc, w_ref, o_ref, side_ref, hn_ref, *, tm, tpb, tc, tn):
    i = pl.program_id(0)
    j = pl.program_id(1)

    @pl.when(j == 0)
    def _():
        _norm_mod_rows(x_ref, g_ref, shl, scl, shc, scc, hn_ref, (i % tpb) * tm, tc)

    acc = jnp.dot(hn_ref[...], w_ref[...], preferred_element_type=F32)
    o_ref[...] = acc.astype(o_ref.dtype)

    @pl.when(j == P_KR // tn)
    def _():
        off = P_KR % tn
        side_ref[...] = acc[:, off:off + P_SIDE]


def _in_proj(xa, gain, mod, w, *, s, tc, nb):
    n, d = xa.shape
    nout = w.shape[1]
    tm = _row_tile(s, 1152)
    tn = _col_tile(nout, 1024)
    assert P_KR // tn == (P_KR + P_SIDE - 1) // tn
    tpb = s // tm
    return pl.pallas_call(
        functools.partial(_in_proj_kernel, tm=tm, tpb=tpb, tc=tc, tn=tn),
        grid=(n // tm, nout // tn),
        in_specs=[
            pl.BlockSpec((tm, d), lambda i, j: (i, 0)),
            pl.BlockSpec((1, d), lambda i, j: (0, 0)),
            *_mod_specs(d, tpb, nb, (0, 1)),
            pl.BlockSpec((d, tn), lambda i, j: (0, j)),
        ],
        out_specs=[
            pl.BlockSpec((tm, tn), lambda i, j: (i, j)),
            pl.BlockSpec((tm, P_SIDE), lambda i, j: (i, 0)),
        ],
        out_shape=[jax.ShapeDtypeStruct((n, nout), BF16), jax.ShapeDtypeStruct((n, P_SIDE), F32)],
        scratch_shapes=[pltpu.VMEM((tm, d), BF16)],
        compiler_params=_cparams(("arbitrary", "arbitrary")),
        name="in_proj",
    )(xa, gain, mod, mod, mod, mod, w)


def _mla_prep_kernel(cq_ref, ckv_ref, kr_ref, cos_ref, sin_ref, gq_ref, gkv_ref, wq_ref, wk_ref, wvt_ref,
                     q_ref, k_ref, vt_ref):
    cos = cos_ref[...]
    sin = sin_ref[...]
    quarter = MLA_ROPE_DIM // 4

    def rope(x):
        return x * cos + _rot_half(x, quarter) * sin

    qn = _rms(cq_ref[...].astype(F32), gq_ref[...]).astype(BF16)
    q = jnp.dot(qn, wq_ref[...], preferred_element_type=F32)
    kvn = _rms(ckv_ref[...].astype(F32), gkv_ref[...]).astype(BF16)
    kn = jnp.dot(kvn, wk_ref[...], preferred_element_type=F32)
    vt = lax.dot_general(wvt_ref[...], kvn, (((1,), (1,)), ((), ())), preferred_element_type=F32)
    vt_ref[...] = vt.astype(BF16)
    kr = rope(kr_ref[...]).astype(BF16)
    for h in range(MLA_HEADS):
        a = h * MLA_QK_PAD
        q_ref[:, a:a + LANES] = (q[:, a:a + LANES] * (MLA_SCALE * LOG2E)).astype(BF16)
        q_ref[:, a + LANES:a + 2 * LANES] = (rope(q[:, a + LANES:a + 2 * LANES])
                                             * (MLA_SCALE * LOG2E)).astype(BF16)
        k_ref[:, a:a + LANES] = kn[:, h * LANES:(h + 1) * LANES].astype(BF16)
        k_ref[:, a + LANES:a + 2 * LANES] = kr


def _mla_prep(proj, side, cos_t, sin_t, g_q, g_kv, wq, wk, wvt, *, s):
    n = proj.shape[0]
    tm = _row_tile(s, 768)
    tpb = s // tm
    hq = MLA_HEADS * MLA_QK_PAD
    hv = MLA_HEADS * MLA_V_DIM
    full = lambda a: pl.BlockSpec(a.shape, lambda i: (0,) * a.ndim)
    return pl.pallas_call(
        _mla_prep_kernel,
        grid=(n // tm,),
        in_specs=[
            pl.BlockSpec((tm, MLA_Q_LORA), lambda i: (i, P_CQ // MLA_Q_LORA)),
            pl.BlockSpec((tm, MLA_KV_LORA), lambda i: (i, P_CKV // MLA_KV_LORA)),
            pl.BlockSpec((tm, LANES), lambda i: (i, 0)),
            pl.BlockSpec((tm, LANES), lambda i: (i % tpb, 0)),
            pl.BlockSpec((tm, LANES), lambda i: (i % tpb, 0)),
            full(g_q), full(g_kv), full(wq), full(wk), full(wvt),
        ],
        out_specs=[
            pl.BlockSpec((tm, hq), lambda i: (i, 0)),
            pl.BlockSpec((tm, hq), lambda i: (i, 0)),
            pl.BlockSpec((hv, tm), lambda i: (0, i)),
        ],
        out_shape=[
            jax.ShapeDtypeStruct((n, hq), BF16),
            jax.ShapeDtypeStruct((n, hq), BF16),
            jax.ShapeDtypeStruct((hv, n), BF16),
        ],
        compiler_params=_cparams(("arbitrary",)),
        name="mla_prep",
    )(proj, proj, side, cos_t, sin_t, g_q, g_kv, wq, wk, wvt)


def _softplus(x):
    return jnp.maximum(x, 0.0) + jnp.log1p(jnp.exp(-jnp.abs(x)))


def _aux_prep_kernel(dq_ref, dk_ref, dv_ref, ab_ref, cos_ref, sin_ref, alog_ref, dtb_ref,
                     q_ref, k_ref, vt_ref, gates_ref, gt_ref, *, tm):
    cos = cos_ref[...]
    sin = sin_ref[...]
    quarter = DIFF_QK_DIM // 4
    for h in range(DIFF_HEADS):
        sl = slice(h * LANES, (h + 1) * LANES)
        xq = dq_ref[:, sl].astype(F32)
        xk = dk_ref[:, sl].astype(F32)
        q_ref[:, sl] = ((xq * cos + _rot_half(xq, quarter) * sin) * (DIFF_SCALE * LOG2E)).astype(BF16)
        k_ref[:, sl] = (xk * cos + _rot_half(xk, quarter) * sin).astype(BF16)
    vt_ref[...] = dv_ref[...].astype(F32).T.astype(BF16)

    ab = ab_ref[...]
    g = -jnp.exp(alog_ref[...]) * _softplus(ab + dtb_ref[...])
    beta = jax.nn.sigmoid(ab)
    row = lax.broadcasted_iota(jnp.int32, (tm, LANES), 0) % GDN_CHUNK
    pre = g
    suf = g
    step = 1
    while step < GDN_CHUNK:
        pre = pre + jnp.where(row >= step, pltpu.roll(pre, step, 0), 0.0)
        suf = suf + jnp.where(row < GDN_CHUNK - step, pltpu.roll(suf, tm - step, 0), 0.0)
        step *= 2
    lane = lax.broadcasted_iota(jnp.int32, (tm, LANES), 1)
    total = pltpu.roll(pre + suf - g, 4 * GDN_HEADS, 1)
    gates = jnp.where(lane < GDN_HEADS, pre,
                      jnp.where(lane < 2 * GDN_HEADS, suf, jnp.where(lane < 4 * GDN_HEADS, beta, total)))
    gates_ref[...] = gates
    gt_ref[...] = gates.T[0:gt_ref.shape[0], :]


def _aux_prep(proj, side, cos_t, sin_t, alog_row, dtb_row, *, s):
    n = proj.shape[0]
    tm = _row_tile(s, 768)
    tpb = s // tm
    wq = DIFF_HEADS * 2 * DIFF_QK_DIM
    wv = DIFF_HEADS * DIFF_V_DIM
    gt_rows = 32
    return pl.pallas_call(
        functools.partial(_aux_prep_kernel, tm=tm),
        grid=(n // tm,),
        in_specs=[
            pl.BlockSpec((tm, wq), lambda i: (i, P_DQ // wq)),
            pl.BlockSpec((tm, wq), lambda i: (i, P_DK // wq)),
            pl.BlockSpec((tm, wv), lambda i: (i, P_DV // wv)),
            pl.BlockSpec((tm, LANES), lambda i: (i, (P_GAB - P_KR) // LANES)),
            pl.BlockSpec((tm, LANES), lambda i: (i % tpb, 0)),
            pl.BlockSpec((tm, LANES), lambda i: (i % tpb, 0)),
            pl.BlockSpec((1, LANES), lambda i: (0, 0)),
            pl.BlockSpec((1, LANES), lambda i: (0, 0)),
        ],
        out_specs=[
            pl.BlockSpec((tm, wq), lambda i: (i, 0)),
            pl.BlockSpec((tm, wq), lambda i: (i, 0)),
            pl.BlockSpec((wv, tm), lambda i: (0, i)),
            pl.BlockSpec((tm, LANES), lambda i: (i, 0)),
            pl.BlockSpec((gt_rows, tm), lambda i: (0, i)),
        ],
        out_shape=[
            jax.ShapeDtypeStruct((n, wq), BF16),
            jax.ShapeDtypeStruct((n, wq), BF16),
            jax.ShapeDtypeStruct((wv, n), BF16),
            jax.ShapeDtypeStruct((n, LANES), F32),
            jax.ShapeDtypeStruct((gt_rows, n), F32),
        ],
        compiler_params=_cparams(("arbitrary",)),
        name="aux_prep",
    )(proj, proj, proj, side, cos_t, sin_t, alog_row, dtb_row)


def _gdn_prep_kernel(x_ref, w_ref, o_ref, *, s, tc, groups):
    c = pl.program_id(1)
    x = x_ref[...].astype(F32)
    w = w_ref[...]
    t = lax.broadcasted_iota(jnp.int32, (s, 1), 0)
    lo = jnp.where(t < tc, 0, tc)
    hi = jnp.where(t < tc, tc, s)
    half = GDN_CONV // 2
    acc = x * w[half:half + 1, :]
    for off in range(-half, half + 1):
        if off == 0:
            continue
        xs = pltpu.roll(x, (-off) % s, 0)
        ok = (t + off >= lo) & (t + off < hi)
        acc = acc + jnp.where(ok, xs, 0.0) * w[off + half:off + half + 1, :]
    y = _silu(acc)
    qscale = jnp.where(c < groups, GDN_K_DIM ** -0.5, 1.0)
    is_qk = c < 2 * groups
    for gi in range(x.shape[1] // LANES):
        sl = slice(gi * LANES, (gi + 1) * LANES)
        yg = y[:, sl]
        inv = lax.rsqrt(jnp.sum(yg * yg, axis=-1, keepdims=True) + NORM_EPS) * qscale
        o_ref[:, sl] = yg * jnp.where(is_qk, inv, 1.0)


def _gdn_prep(proj, conv_w, *, s, tc, nb):
    n = proj.shape[0]
    cw = 3 * LANES
    groups = GDN_HEADS * GDN_K_DIM // cw
    off = P_GQKV // cw
    return pl.pallas_call(
        functools.partial(_gdn_prep_kernel, s=s, tc=tc, groups=groups),
        grid=(nb, GQKV_W // cw),
        in_specs=[
            pl.BlockSpec((s, cw), lambda b, c: (b, off + c)),
            pl.BlockSpec((GDN_CONV, cw), lambda b, c: (0, c)),
        ],
        out_specs=pl.BlockSpec((s, cw), lambda b, c: (b, c)),
        out_shape=jax.ShapeDtypeStruct((n, GQKV_W), F32),
        compiler_params=_cparams(("arbitrary", "arbitrary")),
        name="gdn_prep",
    )(proj, conv_w)


def _gdn_solve_kernel(q_ref, k_ref, v_ref, gates_ref, gt_ref, u_ref, w_ref, qg_ref, kd_ref, in_ref,
                      l_scr, x_scr, rhs_scr):
    t = q_ref.shape[0]
    c = GDN_CHUNK
    nh = GDN_HEADS
    ii = lax.broadcasted_iota(jnp.int32, (t, t), 0)
    jj = lax.broadcasted_iota(jnp.int32, (t, t), 1)
    log2c = c.bit_length() - 1
    same = (ii >> log2c) == (jj >> log2c)
    eye = jnp.where(ii == jj, 1.0, 0.0)
    level_masks = [((ii >> (sh + 1)) == (jj >> (sh + 1))) & ((ii >> sh) != (jj >> sh))
                   for sh in range(1, log2c)]
    pair = (ii >> 1) == (jj >> 1)
    gates = gates_ref[...]
    gt = gt_ref[...]
    nt = (((1,), (1,)), ((), ()))
    for h in range(nh):
        sl = slice(h * LANES, (h + 1) * LANES)
        q = q_ref[:, sl]
        k = k_ref[:, sl]
        v = v_ref[:, sl]
        kb = k.astype(BF16)
        kk = lax.dot_general(kb, kb, nt, preferred_element_type=F32)
        qk = lax.dot_general(q.astype(BF16), kb, nt, preferred_element_type=F32)
        for d in range(2):
            fwd = d == 0
            gi = d * nh + h
            gcol = gates[:, gi:gi + 1]
            bcol = gates[:, 2 * nh + gi:2 * nh + gi + 1]
            glcol = gates[:, 4 * nh + gi:4 * nh + gi + 1]
            grow = gt[gi:gi + 1, :]
            incl = same & ((ii >= jj) if fwd else (ii <= jj))
            strict = same & ((ii > jj) if fwd else (ii < jj))
            decay = jnp.where(incl, jnp.exp(jnp.where(incl, gcol - grow, 0.0)), 0.0)
            lmat = jnp.where(strict, bcol * kk * decay, 0.0)
            l_scr[gi] = lmat.astype(BF16)
            x_scr[gi] = (eye - jnp.where(pair, lmat, 0.0)).astype(BF16)
            eg = jnp.exp(gcol)
            rhs_scr[gi] = jnp.concatenate([v * bcol, k * (bcol * eg)], axis=1).astype(BF16)
            cs = slice(gi * LANES, (gi + 1) * LANES)
            qg_ref[:, cs] = (q * eg).astype(BF16)
            kd_ref[:, cs] = (k * jnp.exp(glcol - gcol)).astype(BF16)
            intra = qk * decay
            compact = jnp.concatenate([intra[n * c:(n + 1) * c, n * c:(n + 1) * c] for n in range(t // c)],
                                      axis=0)
            in_ref[:, cs] = jnp.concatenate([compact, jnp.zeros((t, LANES - c), F32)], axis=1).astype(BF16)
    zero = jnp.zeros((t, t), BF16)
    for m in level_masks:
        for gi in range(2 * nh):
            xb = x_scr[gi]
            nx = jnp.dot(jnp.where(m, l_scr[gi], zero), xb, preferred_element_type=F32)
            z = jnp.dot(xb, nx.astype(BF16), preferred_element_type=F32)
            x_scr[gi] = xb - z.astype(BF16)
    for gi in range(2 * nh):
        uw = jnp.dot(x_scr[gi], rhs_scr[gi], preferred_element_type=F32)
        cs = slice(gi * LANES, (gi + 1) * LANES)
        u_ref[:, cs] = uw[:, :GDN_V_DIM]
        w_ref[:, cs] = uw[:, GDN_V_DIM:].astype(BF16)


def _gdn_solve(qkv, gates, gt, *, s, nb):
    n = qkv.shape[0]
    t = GDN_TILE
    nt = s // t
    hw = GDN_HEADS * GDN_K_DIM
    cw = 2 * hw
    row = lambda b, i: (b * nt + i, 0)
    return pl.pallas_call(
        _gdn_solve_kernel,
        grid=(nb, nt),
        in_specs=[
            pl.BlockSpec((t, hw), lambda b, i: (b * nt + i, 0)),
            pl.BlockSpec((t, hw), lambda b, i: (b * nt + i, 1)),
            pl.BlockSpec((t, hw), lambda b, i: (b * nt + i, 2)),
            pl.BlockSpec((t, LANES), row),
            pl.BlockSpec((gt.shape[0], t), lambda b, i: (0, b * nt + i)),
        ],
        out_specs=[pl.BlockSpec((t, cw), row)] * 5,
        out_shape=[jax.ShapeDtypeStruct((n, cw), F32)] + [jax.ShapeDtypeStruct((n, cw), BF16)] * 4,
        scratch_shapes=[
            pltpu.VMEM((2 * GDN_HEADS, t, t), BF16),
            pltpu.VMEM((2 * GDN_HEADS, t, t), BF16),
            pltpu.VMEM((2 * GDN_HEADS, t, GDN_V_DIM + GDN_K_DIM), BF16),
        ],
        compiler_params=_cparams(("arbitrary", "arbitrary")),
        name="gdn_solve",
    )(qkv, qkv, qkv, gates, gt)


def _gdn_scan_kernel(uf, ub, wf, wb, qgf, qgb, kdf, kdb, inf, inb, gf_ref, gb_ref, of_ref, ob_ref, s_ref):
    @pl.when(pl.program_id(1) == 0)
    def _():
        s_ref[...] = jnp.zeros_like(s_ref)

    c = GDN_CHUNK
    nh = GDN_HEADS
    npt = uf.shape[0] // c
    tn = (((0,), (0,)), ((), ()))
    dirs = ((uf, wf, qgf, kdf, inf, gf_ref, of_ref), (ub, wb, qgb, kdb, inb, gb_ref, ob_ref))
    for p in range(npt):
        chains = []
        for d, (u_ref, w_ref, qg_ref, kd_ref, in_ref, g_ref, o_ref) in enumerate(dirs):
            pc = p if d == 0 else npt - 1 - p
            rs = slice(pc * c, (pc + 1) * c)
            totals = jnp.exp(g_ref[pc * c:pc * c + 1, :])
            for h in range(nh):
                j = d * nh + h
                cs = slice(h * LANES, (h + 1) * LANES)
                chains.append((j, rs, cs, u_ref, w_ref, qg_ref, kd_ref, in_ref, o_ref,
                               totals[:, 4 * nh + j:4 * nh + j + 1]))
        states = [s_ref[ch[0]] for ch in chains]
        prods = []
        for st, (j, rs, cs, u_ref, w_ref, qg_ref, kd_ref, in_ref, o_ref, decay) in zip(states, chains):
            wq = jnp.concatenate([w_ref[rs, cs], qg_ref[rs, cs]], axis=0)
            prods.append(jnp.dot(wq, st.astype(BF16), preferred_element_type=F32))
        v_news = [(ch[3][ch[1], ch[2]] - r[:c]).astype(BF16) for r, ch in zip(prods, chains)]
        for st, r, v_new, (j, rs, cs, u_ref, w_ref, qg_ref, kd_ref, in_ref, o_ref, decay) in zip(
                states, prods, v_news, chains):
            intra = in_ref[rs, cs][:, :c]
            o_ref[rs, cs] = r[c:] + jnp.dot(intra, v_new, preferred_element_type=F32)
            s_ref[j] = st * decay + lax.dot_general(kd_ref[rs, cs], v_new, tn, preferred_element_type=F32)


def _gdn_scan(u, w, qg, kd, intra, gates, *, s, tc, nb):
    n = u.shape[0]
    t = GDN_TILE
    nt = s // t
    nct = tc // t
    hw = GDN_HEADS * GDN_V_DIM

    def bwd_tile(i):
        return jnp.where(i < nct, nct - 1 - i, nct + nt - 1 - i)

    fwd = lambda b, i: (b * nt + i, 0)
    bwd = lambda b, i: (b * nt + bwd_tile(i), 1)
    pair = [pl.BlockSpec((t, hw), fwd), pl.BlockSpec((t, hw), bwd)]
    return pl.pallas_call(
        _gdn_scan_kernel,
        grid=(nb, nt),
        in_specs=pair * 5 + [
            pl.BlockSpec((t, LANES), fwd),
            pl.BlockSpec((t, LANES), lambda b, i: (b * nt + bwd_tile(i), 0)),
        ],
        out_specs=[
            pl.BlockSpec((t, hw), fwd),
            pl.BlockSpec((t, hw), lambda b, i: (b * nt + bwd_tile(i), 0)),
        ],
        out_shape=[jax.ShapeDtypeStruct((n, hw), F32)] * 2,
        scratch_shapes=[pltpu.VMEM((2 * GDN_HEADS, GDN_K_DIM, GDN_V_DIM), F32)],
        compiler_params=_cparams(("arbitrary", "arbitrary")),
        name="gdn_scan",
    )(u, u, w, w, qg, qg, kd, kd, intra, intra, gates, gates)


def _gdn_out_kernel(of_ref, ob_ref, z_ref, gain_ref, o_ref):
    gain = gain_ref[...]
    for h in range(GDN_HEADS):
        sl = slice(h * LANES, (h + 1) * LANES)
        o = of_ref[:, sl] + ob_ref[:, sl]
        o_ref[:, sl] = (_rms(o, gain) * _silu(z_ref[:, sl].astype(F32))).astype(o_ref.dtype)


def _gdn_out(o_f, o_b, proj, gain, *, s):
    n, hw = o_f.shape
    tm = _row_tile(s, 768)
    return pl.pallas_call(
        _gdn_out_kernel,
        grid=(n // tm,),
        in_specs=[
            pl.BlockSpec((tm, hw), lambda i: (i, 0)),
            pl.BlockSpec((tm, hw), lambda i: (i, 0)),
            pl.BlockSpec((tm, hw), lambda i: (i, P_GZ // hw)),
            pl.BlockSpec((1, LANES), lambda i: (0, 0)),
        ],
        out_specs=pl.BlockSpec((tm, hw), lambda i: (i, 0)),
        out_shape=jax.ShapeDtypeStruct((n, hw), BF16),
        compiler_params=_cparams(("arbitrary",)),
        name="gdn_out",
    )(o_f, o_b, proj, gain)


def _softmax2_t(st):
    m = jnp.max(st, axis=0, keepdims=True)
    e = jnp.exp2(st - m)
    return e, 1.0 / jnp.sum(e, axis=0, keepdims=True)


def _pipelined_query_tiles(nq, scores, finish):
    for t in range(min(ATTN_AHEAD, nq)):
        scores(t)
    for t in range(nq):
        if t + ATTN_AHEAD < nq:
            scores(t + ATTN_AHEAD)
        finish(t)


def _mla_attn_kernel(q_ref, k_ref, vt_ref, o_ref, s_scr, *, tc, tq):
    nt = (((1,), (1,)), ((), ()))
    s = q_ref.shape[0]
    keys = lambda t: tc if t * tq < tc else s

    def scores(t):
        kn = keys(t)
        s_scr[t % ATTN_BUFS, 0:kn, :] = lax.dot_general(k_ref[0:kn, :], q_ref[t * tq:(t + 1) * tq, :], nt,
                                                preferred_element_type=F32)

    def finish(t):
        kn = keys(t)
        e, rinv = _softmax2_t(s_scr[t % ATTN_BUFS, 0:kn, :])
        ot = jnp.dot(vt_ref[:, 0:kn], e.astype(BF16), preferred_element_type=F32) * rinv
        o_ref[t * tq:(t + 1) * tq, :] = ot.T.astype(o_ref.dtype)

    _pipelined_query_tiles(s // tq, scores, finish)


def _mla_attn(q, k, vt, *, s, tc, nb):
    n = q.shape[0]
    tq = _row_tile(tc, 256)
    return pl.pallas_call(
        functools.partial(_mla_attn_kernel, tc=tc, tq=tq),
        grid=(nb, MLA_HEADS),
        in_specs=[
            pl.BlockSpec((s, MLA_QK_PAD), lambda b, h: (b, h)),
            pl.BlockSpec((s, MLA_QK_PAD), lambda b, h: (b, h)),
            pl.BlockSpec((MLA_V_DIM, s), lambda b, h: (h, b)),
        ],
        out_specs=pl.BlockSpec((s, MLA_V_DIM), lambda b, h: (b, h)),
        out_shape=jax.ShapeDtypeStruct((n, MLA_HEADS * MLA_V_DIM), BF16),
        scratch_shapes=[pltpu.VMEM((ATTN_BUFS, s, tq), F32)],
        compiler_params=_cparams(("arbitrary", "arbitrary")),
        name="mla_attn",
    )(q, k, vt)


def _diff_attn_kernel(q_ref, k_ref, vt_ref, lam_ref, gain_ref, o_ref, s_scr, *, tc, tq, lam_init):
    nt = (((1,), (1,)), ((), ()))
    s = q_ref.shape[0]
    keys = lambda t: tc if t * tq < tc else s
    lp = lam_ref[...]
    lam = (jnp.exp(jnp.sum(lp[0:1] * lp[1:2], axis=-1, keepdims=True))
           - jnp.exp(jnp.sum(lp[2:3] * lp[3:4], axis=-1, keepdims=True)) + lam_init)
    gain = gain_ref[...] * (1.0 - lam_init)
    lane = lax.broadcasted_iota(jnp.int32, (tq, LANES), 1)

    def scores(t):
        kn = keys(t)
        q = q_ref[t * tq:(t + 1) * tq, :]
        zero = jnp.zeros_like(q)
        for g in range(2):
            qg = jnp.where((lane < DIFF_QK_DIM) == (g == 0), q, zero)
            s_scr[2 * (t % ATTN_BUFS) + g, 0:kn, :] = lax.dot_general(k_ref[0:kn, :], qg, nt,
                                                              preferred_element_type=F32)

    def finish(t):
        kn = keys(t)
        e1, r1 = _softmax2_t(s_scr[2 * (t % ATTN_BUFS), 0:kn, :])
        e2, r2 = _softmax2_t(s_scr[2 * (t % ATTN_BUFS) + 1, 0:kn, :])
        p = e1 - e2 * (r2 * lam / r1)
        ot = jnp.dot(vt_ref[:, 0:kn], p.astype(BF16), preferred_element_type=F32) * r1
        ms = jnp.mean(ot * ot, axis=0, keepdims=True)
        ot = ot * lax.rsqrt(ms + NORM_EPS) * gain
        o_ref[t * tq:(t + 1) * tq, :] = ot.T.astype(o_ref.dtype)

    _pipelined_query_tiles(s // tq, scores, finish)


def _diff_attn(q, k, vt, lam_p, gain_col, *, s, tc, nb, lam_init):
    n = q.shape[0]
    tq = _row_tile(tc, 256)
    return pl.pallas_call(
        functools.partial(_diff_attn_kernel, tc=tc, tq=tq, lam_init=lam_init),
        grid=(nb, DIFF_HEADS),
        in_specs=[
            pl.BlockSpec((s, LANES), lambda b, h: (b, h)),
            pl.BlockSpec((s, LANES), lambda b, h: (b, h)),
            pl.BlockSpec((DIFF_V_DIM, s), lambda b, h: (h, b)),
            pl.BlockSpec(lam_p.shape, lambda b, h: (0, 0)),
            pl.BlockSpec(gain_col.shape, lambda b, h: (0, 0)),
        ],
        out_specs=pl.BlockSpec((s, DIFF_V_DIM), lambda b, h: (b, h)),
        out_shape=jax.ShapeDtypeStruct((n, DIFF_HEADS * DIFF_V_DIM), BF16),
        scratch_shapes=[pltpu.VMEM((2 * ATTN_BUFS, s, tq), F32)],
        compiler_params=_cparams(("arbitrary", "arbitrary")),
        name="diff_attn",
    )(q, k, vt, lam_p, gain_col)


def _out_proj_kernel(x_ref, a1_ref, a2_ref, a3_ref, w1_ref, w2_ref, w3_ref, gl_ref, gc_ref, o_ref,
                     *, tm, tpb, tc):
    i = pl.program_id(0)
    acc = jnp.dot(a1_ref[...], w1_ref[...], preferred_element_type=F32)
    acc += jnp.dot(a2_ref[...], w2_ref[...], preferred_element_type=F32)
    acc += jnp.dot(a3_ref[...], w3_ref[...], preferred_element_type=F32)
    gate = jnp.where(_ctx_rows(i, tpb, tm, tc), gc_ref[...], gl_ref[...])
    o_ref[...] = x_ref[...] + gate * acc


def _out_proj(xa, a1, a2, a3, w, mod, *, s, tc, nb):
    n, d = xa.shape
    tm = _row_tile(s, 1152)
    tn = _col_tile(d, 1024)
    tpb = s // tm
    k1, k2, k3 = a1.shape[1], a2.shape[1], a3.shape[1]
    assert k1 == k2 and (k1 + k2) % k3 == 0
    return pl.pallas_call(
        functools.partial(_out_proj_kernel, tm=tm, tpb=tpb, tc=tc),
        grid=(n // tm, d // tn),
        in_specs=[
            pl.BlockSpec((tm, tn), lambda i, j: (i, j)),
            pl.BlockSpec((tm, k1), lambda i, j: (i, 0)),
            pl.BlockSpec((tm, k2), lambda i, j: (i, 0)),
            pl.BlockSpec((tm, k3), lambda i, j: (i, 0)),
            pl.BlockSpec((k1, tn), lambda i, j: (0, j)),
            pl.BlockSpec((k2, tn), lambda i, j: (1, j)),
            pl.BlockSpec((k3, tn), lambda i, j: ((k1 + k2) // k3, j)),
            pl.BlockSpec((None, None, 1, tn), lambda i, j: (i // tpb, 2, 0, j)),
            pl.BlockSpec((None, None, 1, tn), lambda i, j: (nb, 2, 0, j)),
        ],
        out_specs=pl.BlockSpec((tm, tn), lambda i, j: (i, j)),
        out_shape=jax.ShapeDtypeStruct((n, d), F32),
        compiler_params=_cparams(("arbitrary", "arbitrary")),
        name="out_proj",
    )(xa, a1, a2, a3, w, w, w, mod, mod)


def _mlp_kernel(x_ref, g_ref, shl, scl, shc, scc, gl_ref, gc_ref, w1_ref, w2_ref, o_ref, hn_ref,
                *, tm, tpb, tc):
    i = pl.program_id(0)
    j = pl.program_id(1)
    first_row = (i % tpb) * tm

    tf = w1_ref.shape[1]
    d = o_ref.shape[1]
    cf = math.gcd(tf, MLP_CHUNK)
    cd = math.gcd(d, MLP_CHUNK)

    def ff_step(first):
        hn = hn_ref[...]
        acts = []
        for c0 in range(0, tf, cf):
            a = jnp.dot(hn, w1_ref[:, c0:c0 + cf], preferred_element_type=F32)
            acts.append(jnp.square(jnp.maximum(a, 0.0)).astype(BF16))
        act = jnp.concatenate(acts, axis=1) if len(acts) > 1 else acts[0]
        for n0 in range(0, d, cd):
            part = jnp.dot(act, w2_ref[:, n0:n0 + cd], preferred_element_type=F32)
            if first:
                o_ref[:, n0:n0 + cd] = part
            else:
                o_ref[:, n0:n0 + cd] += part

    @pl.when(j == 0)
    def _():
        _norm_mod_rows(x_ref, g_ref, shl, scl, shc, scc, hn_ref, first_row, tc)
        ff_step(True)

    @pl.when(j > 0)
    def _():
        ff_step(False)

    @pl.when(j == pl.num_programs(1) - 1)
    def _():
        _gated_residual_rows(x_ref, o_ref, gl_ref, gc_ref, first_row, tc)


def _mlp(xa, gain, mod, w1, w2, *, s, tc, nb, n_rows):
    d = xa.shape[1]
    dff = w1.shape[1]
    tm = _row_tile(s, 768)
    tf = _col_tile(dff, 1024)
    tpb = s // tm
    mod_gate = lambda row: pl.BlockSpec((None, None, 1, d), row)
    return pl.pallas_call(
        functools.partial(_mlp_kernel, tm=tm, tpb=tpb, tc=tc),
        grid=(n_rows // tm, dff // tf),
        in_specs=[
            pl.BlockSpec((tm, d), lambda i, j: (i, 0)),
            pl.BlockSpec((1, d), lambda i, j: (0, 0)),
            *_mod_specs(d, tpb, nb, (3, 4)),
            mod_gate(lambda i, j: (i // tpb, 5, 0, 0)),
            mod_gate(lambda i, j: (nb, 5, 0, 0)),
            pl.BlockSpec((d, tf), lambda i, j: (0, j)),
            pl.BlockSpec((tf, d), lambda i, j: (j, 0)),
        ],
        out_specs=pl.BlockSpec((tm, d), lambda i, j: (i, 0)),
        out_shape=jax.ShapeDtypeStruct((n_rows, d), F32),
        scratch_shapes=[pltpu.VMEM((tm, d), BF16)],
        compiler_params=_cparams(("arbitrary", "arbitrary")),
        name="mlp",
    )(xa, gain, mod, mod, mod, mod, mod, mod, w1, w2)


def _final_norm_kernel(x_ref, g_ref, o_ref):
    o_ref[...] = _rms(x_ref[...], g_ref[...])


def _final_norm(xa, gain, *, s, tc, nb):
    d = xa.shape[1]
    t = s - tc
    tr = _row_tile(math.gcd(tc, t), 512)
    per_b = s // tr
    skip = tc // tr
    nt = t // tr
    return pl.pallas_call(
        _final_norm_kernel,
        grid=(nb, nt),
        in_specs=[
            pl.BlockSpec((tr, d), lambda b, r: (b * per_b + skip + r, 0)),
            pl.BlockSpec((1, d), lambda b, r: (0, 0)),
        ],
        out_specs=pl.BlockSpec((tr, d), lambda b, r: (b * nt + r, 0)),
        out_shape=jax.ShapeDtypeStruct((nb * t, d), F32),
        compiler_params=_cparams(("arbitrary", "arbitrary")),
        name="final_norm",
    )(xa, gain)


def _rope_tables(t, tc, dim, reps):
    rows = t // GRID_W
    row_pos = jnp.repeat(jnp.arange(rows, dtype=F32), GRID_W)
    col_pos = jnp.tile(jnp.arange(GRID_W, dtype=F32), rows)
    quarter = dim // 4
    inv_freq = ROPE_BASE ** (-jnp.arange(quarter, dtype=F32) / quarter)
    ang_r = row_pos[:, None] * inv_freq[None, :]
    ang_c = col_pos[:, None] * inv_freq[None, :]
    ang = jnp.concatenate([ang_r, ang_r, ang_c, ang_c], axis=-1)
    cos = jnp.tile(jnp.cos(ang), (1, reps))
    sin = jnp.tile(jnp.sin(ang), (1, reps))
    pad = LANES - cos.shape[1]
    cos = jnp.pad(cos, ((0, 0), (0, pad)), constant_values=1.0)
    sin = jnp.pad(sin, ((0, 0), (0, pad)))
    cos = jnp.concatenate([jnp.ones((tc, LANES), F32), cos], axis=0)
    sin = jnp.concatenate([jnp.zeros((tc, LANES), F32), sin], axis=0)
    return cos, sin


def _permute_w_in(w):
    d = w.shape[0]
    sizes = (MLA_Q_LORA, MLA_KV_LORA, MLA_ROPE_DIM, GDN_HEADS * GDN_K_DIM, GDN_HEADS * GDN_K_DIM,
             GDN_HEADS * GDN_V_DIM, GDN_HEADS * GDN_V_DIM, 4 * GDN_HEADS,
             DIFF_HEADS * 2 * DIFF_QK_DIM, DIFF_HEADS * 2 * DIFF_QK_DIM, DIFF_HEADS * DIFF_V_DIM)
    offs = [0]
    for sz in sizes:
        offs.append(offs[-1] + sz)
    cq, ckv, kr, gq, gk, gv, gz, gab, dq, dk, dv = (w[:, offs[i]:offs[i + 1]] for i in range(len(sizes)))
    zpad = lambda a: jnp.pad(a, ((0, 0), (0, LANES - a.shape[1])))
    out = jnp.concatenate([cq, gq, gk, gv, gz, zpad(kr), zpad(gab), ckv, dq, dk, dv], axis=1)
    assert out.shape == (d, P_TOTAL)
    return out.astype(BF16)


def kernel(x, c, ctx, c_ctx, w_ada, b_ada, norm1_g, norm2_g, w_in, mla_q_norm_g, mla_kv_norm_g, mla_w_uq, mla_w_ukv, gdn_conv_w, gdn_a_log, gdn_dt_bias, gdn_norm_g, diff_lambda, diff_norm_g, w_out, w_mlp1, w_mlp2, final_norm_g):
    nb, t, d = x.shape
    tc = ctx.shape[1]
    depth = w_ada.shape[0]
    s = tc + t
    n = nb * s
    assert t % GRID_W == 0 and t % GDN_TILE == 0 and tc % GDN_TILE == 0 and tc % NORM_ROWS == 0

    xa = jnp.concatenate([ctx, x], axis=1).reshape(n, d)

    r = -(-(nb + 1) // 8) * 8
    s_rows = jnp.concatenate([c, c_ctx[None, :], jnp.zeros((r - nb - 1, d), F32)], axis=0)
    mod_all = _adaln(s_rows, w_ada, b_ada).reshape(depth, r, 6, 1, d)

    cos_m, sin_m = _rope_tables(t, tc, MLA_ROPE_DIM, 1)
    cos_d, sin_d = _rope_tables(t, tc, DIFF_QK_DIM, 2)

    for l in range(depth):
        mod = mod_all[l]
        lam_init = 0.8 - 0.6 * math.exp(-0.3 * l)

        w_in_p = _permute_w_in(w_in[l])
        wq = mla_w_uq[l].reshape(MLA_Q_LORA, MLA_HEADS, MLA_NOPE_DIM + MLA_ROPE_DIM)
        wq = jnp.pad(wq, ((0, 0), (0, 0), (0, MLA_QK_PAD - wq.shape[2])))
        wq = wq.reshape(MLA_Q_LORA, MLA_HEADS * MLA_QK_PAD).astype(BF16)
        wkv = mla_w_ukv[l].reshape(MLA_KV_LORA, MLA_HEADS, MLA_NOPE_DIM + MLA_V_DIM)
        wk = wkv[:, :, :MLA_NOPE_DIM].reshape(MLA_KV_LORA, MLA_HEADS * MLA_NOPE_DIM).astype(BF16)
        wvt = wkv[:, :, MLA_NOPE_DIM:].reshape(MLA_KV_LORA, MLA_HEADS * MLA_V_DIM).T.astype(BF16)
        gate_pad = LANES - 2 * GDN_HEADS
        alog_row = jnp.pad(gdn_a_log[l].reshape(1, 2 * GDN_HEADS), ((0, 0), (0, gate_pad)))
        dtb_row = jnp.pad(gdn_dt_bias[l].reshape(1, 2 * GDN_HEADS), ((0, 0), (0, gate_pad)))

        proj, side = _in_proj(xa, norm1_g[l][None, :], mod, w_in_p, s=s, tc=tc, nb=nb)

        mq, mk, mvt = _mla_prep(proj, side, cos_m, sin_m, mla_q_norm_g[l][None, :],
                                mla_kv_norm_g[l][None, :], wq, wk, wvt, s=s)
        mla_o = _mla_attn(mq, mk, mvt, s=s, tc=tc, nb=nb)

        dq, dk, dvt, gates, gt = _aux_prep(proj, side, cos_d, sin_d, alog_row, dtb_row, s=s)
        diff_o = _diff_attn(dq, dk, dvt, diff_lambda[l], diff_norm_g[l][:, None], s=s, tc=tc, nb=nb,
                            lam_init=lam_init)

        qkv = _gdn_prep(proj, gdn_conv_w[l], s=s, tc=tc, nb=nb)
        gu, gw, gqg, gkd, gin = _gdn_solve(qkv, gates, gt, s=s, nb=nb)
        o_f, o_b = _gdn_scan(gu, gw, gqg, gkd, gin, gates, s=s, tc=tc, nb=nb)
        gdn_o = _gdn_out(o_f, o_b, proj, gdn_norm_g[l][None, :], s=s)

        xa = _out_proj(xa, mla_o, gdn_o, diff_o, w_out[l].astype(BF16), mod, s=s, tc=tc, nb=nb)
        xa = _mlp(xa, norm2_g[l][None, :], mod, w_mlp1[l].astype(BF16), w_mlp2[l].astype(BF16),
                  s=s, tc=tc, nb=nb, n_rows=n)

    out = _final_norm(xa, final_norm_g[None, :], s=s, tc=tc, nb=nb)
    return out.reshape(nb, t, d)
```
